```python
import math
import jax, jax.numpy as jnp
from jax import lax
import numpy as np

D_MODEL = 1024
BATCH = 4
SEQ = 8192
DEPTH = 2

N_META = 16
CONV_WIDTH = 512
CONV_GROUPS = 8
CONV_K = 3
GDN_HEADS = 4
GDN_DK = 128
GDN_DV = 128
GDN_CONV_K = 4
GDN_CHUNK = 64
D_MIX = CONV_WIDTH + GDN_HEADS * GDN_DV
QKV_COLS = GDN_HEADS * (2 * GDN_DK + GDN_DV)
PROJ_COLS = 3 * CONV_WIDTH + QKV_COLS + GDN_HEADS * GDN_DV + 2 * GDN_HEADS
PEER_HEADS = 8
PEER_NKEYS = 128
PEER_N_EXPERTS = PEER_NKEYS * PEER_NKEYS
PEER_DKEY = 256
PEER_TOPK = 16
PEER_BLOCK = 256
DEEPNORM_ALPHA = (2.0 * DEPTH) ** 0.25
DEEPNORM_BETA = (8.0 * DEPTH) ** -0.25
LN_EPS = 1e-5
RMS_EPS = 1e-6

kernel_name = "hymba_conv_gdn_peer_deepnorm"


def layer_norm(x, g, b):
    xf = x.astype(jnp.float32)
    mu = xf.mean(-1, keepdims=True)
    var = jnp.square(xf - mu).mean(-1, keepdims=True)
    return ((xf - mu) * lax.rsqrt(var + LN_EPS) * g.astype(jnp.float32) + b.astype(jnp.float32)).astype(x.dtype)


def causal_dwconv(x, w):
    K = w.shape[0]
    L = x.shape[1]
    xp = jnp.pad(x, ((0, 0), (K - 1, 0), (0, 0)))
    y = xp[:, 0:L] * w[0]
    for j in range(1, K):
        y = y + xp[:, j:j + L] * w[j]
    return y


def l2norm(x):
    return x * lax.rsqrt(jnp.sum(x * x, axis=-1, keepdims=True) + RMS_EPS)


def gated_delta_rule_chunked(q, k, v, g, beta):
    Bb, H, T, DK = q.shape
    DV = v.shape[-1]
    C = GDN_CHUNK
    N = T // C
    q = q * (DK ** -0.5)
    q = q.reshape(Bb, H, N, C, DK)
    k = k.reshape(Bb, H, N, C, DK)
    v = v.reshape(Bb, H, N, C, DV)
    g = jnp.cumsum(g.reshape(Bb, H, N, C), axis=-1)
    beta = beta.reshape(Bb, H, N, C)
    causal = jnp.tril(jnp.ones((C, C), dtype=bool))
    strict = jnp.tril(jnp.ones((C, C), dtype=bool), -1)
    gdiff = g[..., :, None] - g[..., None, :]
    decay = jnp.where(causal, jnp.exp(jnp.where(causal, gdiff, 0.0)), 0.0)
    k_beta = k * beta[..., None]
    v_beta = v * beta[..., None]
    A = jnp.where(strict, jnp.einsum('bhncd,bhnsd->bhncs', k_beta, k) * decay, 0.0)
    eye = jnp.eye(C, dtype=q.dtype)
    t_inv = lax.linalg.triangular_solve(eye + A, jnp.broadcast_to(eye, A.shape),
                                        left_side=True, lower=True, unit_diagonal=True)
    u = jnp.einsum('bhncs,bhnsv->bhncv', t_inv, v_beta)
    w = jnp.einsum('bhncs,bhnsd->bhncd', t_inv, k_beta * jnp.exp(g)[..., None])
    attn_intra = jnp.where(causal, jnp.einsum('bhncd,bhnsd->bhncs', q, k) * decay, 0.0)
    g_last = g[..., -1]
    k_dec = k * jnp.exp(g_last[..., None] - g)[..., None]
    q_dec = q * jnp.exp(g)[..., None]

    def step(S, inp):
        q_c, k_c, u_c, w_c, a_c, gl_c = inp
        v_new = u_c - jnp.einsum('bhcd,bhdv->bhcv', w_c, S)
        o_c = jnp.einsum('bhcd,bhdv->bhcv', q_c, S) + jnp.einsum('bhcs,bhsv->bhcv', a_c, v_new)
        S = S * jnp.exp(gl_c)[..., None, None] + jnp.einsum('bhcd,bhcv->bhdv', k_c, v_new)
        return S, o_c

    to_front = lambda t: jnp.moveaxis(t, 2, 0)
    xs = (to_front(q_dec), to_front(k_dec), to_front(u), to_front(w), to_front(attn_intra), to_front(g_last))
    S0 = jnp.zeros((Bb, H, DK, DV), dtype=q.dtype)
    _, o = lax.scan(step, S0, xs)
    return jnp.moveaxis(o, 0, 2).reshape(Bb, H, T, DV)


def hybrid_mixer(x, w_in, conv_w, gdn_conv_w, a_log, dt_bias, gdn_norm_w, w_out):
    Bb, L, _ = x.shape
    proj = x @ w_in
    s0 = CONV_WIDTH
    s1 = 2 * CONV_WIDTH
    s2 = 3 * CONV_WIDTH
    s3 = s2 + QKV_COLS
    s4 = s3 + GDN_HEADS * GDN_DV
    s5 = s4 + GDN_HEADS
    cb, cc, ch, qkv, z, a, b = jnp.split(proj, [s0, s1, s2, s3, s4, s5], axis=-1)

    y_conv = cb * causal_dwconv(cc * ch, conv_w)

    qkv = jax.nn.silu(causal_dwconv(qkv, gdn_conv_w)).astype(jnp.float32)
    q, k, v = jnp.split(qkv, [GDN_HEADS * GDN_DK, 2 * GDN_HEADS * GDN_DK], axis=-1)
    q = l2norm(q.reshape(Bb, L, GDN_HEADS, GDN_DK))
    k = l2norm(k.reshape(Bb, L, GDN_HEADS, GDN_DK))
    v = v.reshape(Bb, L, GDN_HEADS, GDN_DV)
    g = -jnp.exp(a_log.astype(jnp.float32)) * jax.nn.softplus(a.astype(jnp.float32) + dt_bias.astype(jnp.float32))
    beta = jax.nn.sigmoid(b.astype(jnp.float32))
    pad = (-N_META) % GDN_CHUNK
    p4 = ((0, 0), (pad, 0), (0, 0), (0, 0))
    p3 = ((0, 0), (pad, 0), (0, 0))
    qh = jnp.pad(q, p4).transpose(0, 2, 1, 3)
    kh = jnp.pad(k, p4).transpose(0, 2, 1, 3)
    vh = jnp.pad(v, p4).transpose(0, 2, 1, 3)
    gh = jnp.pad(g, p3).transpose(0, 2, 1)
    bh = jnp.pad(beta, p3).transpose(0, 2, 1)
    o = gated_delta_rule_chunked(qh, kh, vh, gh, bh)[:, :, pad:]
    o = o.transpose(0, 2, 1, 3)
    zf = z.astype(jnp.float32).reshape(Bb, L, GDN_HEADS, GDN_DV)
    o = o * lax.rsqrt(jnp.mean(o * o, axis=-1, keepdims=True) + RMS_EPS) * gdn_norm_w.astype(jnp.float32) * jax.nn.silu(zf)
    y_gdn = o.reshape(Bb, L, GDN_HEADS * GDN_DV).astype(x.dtype)

    return jnp.concatenate([y_conv.astype(x.dtype), y_gdn], axis=-1) @ w_out


def peer_ffn(x, w_q, sub_k1, sub_k2, expert_u, expert_v):
    Bb, L, D = x.shape
    T = Bb * L
    n_blk = -(-T // PEER_BLOCK)
    xt = jnp.pad(x.reshape(T, D), ((0, n_blk * PEER_BLOCK - T), (0, 0))).reshape(n_blk, PEER_BLOCK, D)
    half = PEER_DKEY // 2

    def block(xb):
        q = (xb @ w_q).astype(jnp.float32).reshape(PEER_BLOCK, PEER_HEADS, 2, half)
        s1 = jnp.einsum('thd,kd->thk', q[:, :, 0], sub_k1.astype(jnp.float32))
        s2 = jnp.einsum('thd,kd->thk', q[:, :, 1], sub_k2.astype(jnp.float32))
        t1, i1 = lax.top_k(s1, PEER_TOPK)
        t2, i2 = lax.top_k(s2, PEER_TOPK)
        cand = (t1[..., :, None] + t2[..., None, :]).reshape(PEER_BLOCK, PEER_HEADS, PEER_TOPK * PEER_TOPK)
        top, ci = lax.top_k(cand, PEER_TOPK)
        e1 = jnp.take_along_axis(i1, ci // PEER_TOPK, axis=-1)
        e2 = jnp.take_along_axis(i2, ci % PEER_TOPK, axis=-1)
        idx = (e1 * PEER_NKEYS + e2).reshape(PEER_BLOCK, PEER_HEADS * PEER_TOPK)
        gate = jax.nn.softmax(top, axis=-1).reshape(PEER_BLOCK, PEER_HEADS * PEER_TOPK)
        u = expert_u[idx]
        act = jax.nn.gelu(jnp.einsum('td,ted->te', xb, u).astype(jnp.float32), approximate=False)
        coef = (gate * act).astype(xb.dtype)
        v = expert_v[idx]
        return jnp.einsum('te,ted->td', coef, v)

    y = lax.map(block, xt)
    return y.reshape(n_blk * PEER_BLOCK, D)[:T].reshape(Bb, L, D)


def setup_inputs(seed: int = 0) -> dict:
    key = jax.random.key(seed)
    ks = jax.random.split(key, 24)
    f32 = jnp.float32
    nrm = lambda k, shape, s: jax.random.normal(k, shape, f32) * s
    dt = jnp.exp(jax.random.uniform(ks[7], (DEPTH, GDN_HEADS), f32, math.log(1e-3), math.log(1e-1)))
    return {
        "x": jax.random.normal(ks[0], (BATCH, SEQ, D_MODEL), f32),
        "meta_tokens": nrm(ks[1], (N_META, D_MODEL), 1.0),
        "ln_in_g": 1.0 + nrm(ks[2], (D_MODEL,), 0.02),
        "ln_in_b": nrm(ks[3], (D_MODEL,), 0.02),
        "w_in": nrm(ks[4], (DEPTH, D_MODEL, PROJ_COLS), D_MODEL ** -0.5),
        "conv_w": nrm(ks[5], (DEPTH, CONV_K, CONV_WIDTH), CONV_K ** -0.5),
        "gdn_conv_w": nrm(ks[6], (DEPTH, GDN_CONV_K, QKV_COLS), GDN_CONV_K ** -0.5),
        "a_log": jnp.log(jax.random.uniform(ks[8], (DEPTH, GDN_HEADS), f32, 1.0, 16.0)),
        "dt_bias": dt + jnp.log(-jnp.expm1(-dt)),
        "gdn_norm_w": 1.0 + nrm(ks[9], (DEPTH, GDN_DV), 0.02),
        "w_out": nrm(ks[10], (DEPTH, D_MIX, D_MODEL), D_MIX ** -0.5 * DEEPNORM_BETA),
        "ln1_g": 1.0 + nrm(ks[11], (DEPTH, D_MODEL), 0.02),
        "ln1_b": nrm(ks[12], (DEPTH, D_MODEL), 0.02),
        "peer_w_q": nrm(ks[13], (DEPTH, D_MODEL, PEER_HEADS * PEER_DKEY), D_MODEL ** -0.5),
        "peer_k1": nrm(ks[14], (DEPTH, PEER_NKEYS, PEER_DKEY // 2), (PEER_DKEY // 2) ** -0.5),
        "peer_k2": nrm(ks[15], (DEPTH, PEER_NKEYS, PEER_DKEY // 2), (PEER_DKEY // 2) ** -0.5),
        "peer_u": nrm(ks[16], (DEPTH, PEER_N_EXPERTS, D_MODEL), D_MODEL ** -0.5),
        "peer_v": nrm(ks[17], (DEPTH, PEER_N_EXPERTS, D_MODEL), (PEER_HEADS * PEER_TOPK) ** -0.5 * DEEPNORM_BETA),
        "ln2_g": 1.0 + nrm(ks[18], (DEPTH, D_MODEL), 0.02),
        "ln2_b": nrm(ks[19], (DEPTH, D_MODEL), 0.02),
    }


def reference(x, meta_tokens, ln_in_g, ln_in_b, w_in, conv_w, gdn_conv_w, a_log, dt_bias, gdn_norm_w,
              w_out, ln1_g, ln1_b, peer_w_q, peer_k1, peer_k2, peer_u, peer_v, ln2_g, ln2_b):
    Bb = x.shape[0]
    meta = jnp.broadcast_to(meta_tokens[None].astype(x.dtype), (Bb, N_META, D_MODEL))
    h = jnp.concatenate([meta, x], axis=1)
    h = layer_norm(h, ln_in_g, ln_in_b)
    for l in range(DEPTH):
        mix = hybrid_mixer(h, w_in[l], conv_w[l], gdn_conv_w[l], a_log[l], dt_bias[l], gdn_norm_w[l], w_out[l])
        h = layer_norm(DEEPNORM_ALPHA * h + mix, ln1_g[l], ln1_b[l])
        ffn = peer_ffn(h, peer_w_q[l], peer_k1[l], peer_k2[l], peer_u[l], peer_v[l])
        h = layer_norm(DEEPNORM_ALPHA * h + ffn, ln2_g[l], ln2_b[l])
    return h[:, N_META:]
```

```python
import functools

import jax
import jax.numpy as jnp
import numpy as np
from jax import lax
from jax.experimental import pallas as pl
from jax.experimental.pallas import tpu as pltpu

F32 = jnp.float32
BF16 = jnp.bfloat16

N_META = 16
CONV_WIDTH = 512
CONV_K = 3
GDN_HEADS = 4
GDN_DK = 128
GDN_DV = 128
GDN_CONV_K = 4
GDN_CHUNK = 64
QKV_COLS = GDN_HEADS * (2 * GDN_DK + GDN_DV)
GDN_WIDTH = GDN_HEADS * GDN_DV
PEER_HEADS = 8
PEER_NKEYS = 128
PEER_HALF = 128
PEER_TOPK = 16
LN_EPS = 1e-5
RMS_EPS = 1e-6
SEQ_PAD = (-N_META) % GDN_CHUNK
LANES = 128
SUBLANES = 8
VMEM_LIMIT = 56 * 1024 * 1024
NEG_INF = float("-inf")


def _mm(a, b, precision=None):
    return jnp.dot(a, b, preferred_element_type=F32, precision=precision)


def _mm_nt(a, b, precision=None):
    return lax.dot_general(a, b, (((1,), (1,)), ((), ())), preferred_element_type=F32, precision=precision)


def _mm_tn(a, b, precision=None):
    return lax.dot_general(a, b, (((0,), (0,)), ((), ())), preferred_element_type=F32, precision=precision)


def _layer_norm(x, g, b):
    mu = jnp.mean(x, axis=-1, keepdims=True)
    xc = x - mu
    var = jnp.mean(xc * xc, axis=-1, keepdims=True)
    return xc * lax.rsqrt(var + LN_EPS) * g + b


def _silu(x):
    return x * jax.nn.sigmoid(x)


def _pick_block(n, target, mult):
    best = None
    for d in range(mult, min(n, target) + 1, mult):
        if n % d == 0:
            best = d
    assert best is not None, (n, target, mult)
    return best


def _params(*sem):
    return pltpu.CompilerParams(dimension_semantics=sem, vmem_limit_bytes=VMEM_LIMIT)


def _ln_kernel(x_ref, g_ref, b_ref, o_ref):
    o_ref[...] = _layer_norm(x_ref[...], g_ref[...], b_ref[...])


def _ln_call(x2d, g, b, tb):
    n, d = x2d.shape
    return pl.pallas_call(
        _ln_kernel,
        grid=(n // tb,),
        in_specs=[pl.BlockSpec((tb, d), lambda i: (i, 0)),
                  pl.BlockSpec((1, d), lambda i: (0, 0)),
                  pl.BlockSpec((1, d), lambda i: (0, 0))],
        out_specs=pl.BlockSpec((tb, d), lambda i: (i, 0)),
        out_shape=jax.ShapeDtypeStruct((n, d), F32),
        compiler_params=_params("parallel"),
    )(x2d, g, b)


def _causal_conv(u, prev, w):
    k = w.shape[0]

    def taps(a):
        acc = a * w[k - 1:k]
        for s in range(1, k):
            acc = acc + pltpu.roll(a, s, 0) * w[k - 1 - s:k - s]
        return acc

    body = taps(u)
    head = taps(jnp.concatenate([prev, u[:SUBLANES]], axis=0))[SUBLANES:]
    return jnp.concatenate([head, body[SUBLANES:]], axis=0)


def _inproj_kernel(h_ref, wc_ref, wqkv_ref, wz_ref, wab_ref, cw_ref, gcw_ref, alog_ref, dtb_ref,
                   yconv_ref, q_ref, k_ref, v_ref, sz_ref, gb_ref, carry_c, carry_qkv):
    j = pl.program_id(1)
    tb = h_ref.shape[1]

    @pl.when(j == 0)
    def _():
        carry_c[...] = jnp.zeros_like(carry_c)
        carry_qkv[...] = jnp.zeros_like(carry_qkv)

    xb = h_ref[0].astype(BF16)
    row = j * tb + lax.broadcasted_iota(jnp.int32, (tb, 1), 0)
    valid = row >= SEQ_PAD

    p = _mm(xb, wc_ref[...])
    u = jnp.where(valid, p[:, CONV_WIDTH:2 * CONV_WIDTH] * p[:, 2 * CONV_WIDTH:], 0.0)
    yconv_ref[0] = (p[:, :CONV_WIDTH] * _causal_conv(u, carry_c[...], cw_ref[...])).astype(BF16)
    carry_c[...] = u[tb - SUBLANES:]

    p = jnp.where(valid, _mm(xb, wqkv_ref[...]), 0.0)
    s = _silu(_causal_conv(p, carry_qkv[...], gcw_ref[...]))
    carry_qkv[...] = p[tb - SUBLANES:]
    for hh in range(GDN_HEADS):
        lo = hh * GDN_DK
        qh = s[:, lo:lo + GDN_DK]
        kh = s[:, GDN_WIDTH + lo:GDN_WIDTH + lo + GDN_DK]
        q_ref[0, :, lo:lo + GDN_DK] = qh * lax.rsqrt(jnp.sum(qh * qh, axis=-1, keepdims=True) + RMS_EPS)
        k_ref[0, :, lo:lo + GDN_DK] = kh * lax.rsqrt(jnp.sum(kh * kh, axis=-1, keepdims=True) + RMS_EPS)
    v_ref[0] = s[:, 2 * GDN_WIDTH:]

    sz_ref[0] = _silu(_mm(xb, wz_ref[...])).astype(BF16)

    ab = _mm(xb, wab_ref[...])
    t = ab + dtb_ref[...]
    softplus = jnp.maximum(t, 0.0) + jnp.log1p(jnp.exp(-jnp.abs(t)))
    g = -jnp.exp(alog_ref[...]) * softplus
    lane = lax.broadcasted_iota(jnp.int32, (1, LANES), 1)
    gb = jnp.where(lane < GDN_HEADS, g, jax.nn.sigmoid(ab))
    gb_ref[0] = jnp.where(valid, gb, 0.0)


def _inproj_call(h, wc, wqkv, wz, wab, cw, gcw, alog, dtb, tb):
    b, lp, d = h.shape
    full = lambda a: pl.BlockSpec(a.shape, lambda bi, j: (0,) * a.ndim)
    blk = lambda w: pl.BlockSpec((1, tb, w), lambda bi, j: (bi, j, 0))
    out = lambda w, dt: jax.ShapeDtypeStruct((b, lp, w), dt)
    return pl.pallas_call(
        _inproj_kernel,
        grid=(b, lp // tb),
        in_specs=[blk(d), full(wc), full(wqkv), full(wz), full(wab), full(cw), full(gcw), full(alog), full(dtb)],
        out_specs=[blk(CONV_WIDTH), blk(GDN_WIDTH), blk(GDN_WIDTH), blk(GDN_WIDTH), blk(GDN_WIDTH), blk(LANES)],
        out_shape=[out(CONV_WIDTH, BF16), out(GDN_WIDTH, F32), out(GDN_WIDTH, F32), out(GDN_WIDTH, F32),
                   out(GDN_WIDTH, BF16), out(LANES, F32)],
        scratch_shapes=[pltpu.VMEM((SUBLANES, CONV_WIDTH), F32), pltpu.VMEM((SUBLANES, QKV_COLS), F32)],
        compiler_params=_params("parallel", "arbitrary"),
    )(h, wc, wqkv, wz, wab, cw, gcw, alog, dtb)


def _gdn_kernel(q_ref, k_ref, v_ref, gb_ref, o_ref, state, *, chunks):
    c = GDN_CHUNK
    hi = lax.Precision.HIGHEST

    @pl.when(pl.program_id(1) == 0)
    def _():
        state[...] = jnp.zeros_like(state)

    ri = lax.broadcasted_iota(jnp.int32, (c, c), 0)
    ci = lax.broadcasted_iota(jnp.int32, (c, c), 1)
    causal = ri >= ci
    strict = ri > ci
    eye = (ri == ci).astype(F32)
    ltri = causal.astype(F32)
    sel = (lax.broadcasted_iota(jnp.int32, (SUBLANES, LANES), 0)
           == lax.broadcasted_iota(jnp.int32, (SUBLANES, LANES), 1)).astype(F32)
    levels = [jnp.logical_and(jnp.logical_and((ri >> (l + 1)) == (ci >> (l + 1)), ((ri >> l) & 1) == 1),
                              ((ci >> l) & 1) == 0) for l in range(6)]

    for cc in range(chunks):
        rows = slice(cc * c, (cc + 1) * c)
        gbc = gb_ref[0, rows, :]
        gcum = _mm(ltri, gbc, hi)
        gcum_t = _mm_nt(sel, gcum, hi)
        for hh in range(GDN_HEADS):
            lo = hh * GDN_DK
            qh = q_ref[0, rows, lo:lo + GDN_DK] * (GDN_DK ** -0.5)
            kh = k_ref[0, rows, lo:lo + GDN_DK]
            vh = v_ref[0, rows, lo:lo + GDN_DV]
            beta = gbc[:, GDN_HEADS + hh:GDN_HEADS + hh + 1]
            gcol = gcum[:, hh:hh + 1]
            grow = gcum_t[hh:hh + 1, :]
            glast = gcol[c - 1:c, :]
            decay = jnp.where(causal, jnp.exp(jnp.where(causal, gcol - grow, 0.0)), 0.0)
            kbeta = kh * beta
            vbeta = vh * beta
            khb = kh.astype(BF16)
            a = jnp.where(strict, _mm_nt(kbeta.astype(BF16), khb) * decay, 0.0)
            tinv = eye
            for m in levels:
                tinv = tinv - _mm(tinv, _mm(jnp.where(m, a, 0.0), tinv, hi), hi)
            egc = jnp.exp(gcol)
            tb16 = tinv.astype(BF16)
            u = _mm(tb16, vbeta.astype(BF16))
            w = _mm(tb16, (kbeta * egc).astype(BF16))
            attn = jnp.where(causal, _mm_nt(qh.astype(BF16), khb) * decay, 0.0)
            kdec = kh * jnp.exp(glast - gcol)
            qdec = qh * egc
            sh = state[hh]
            sb = sh.astype(BF16)
            vnew = u - _mm(w.astype(BF16), sb)
            vnb = vnew.astype(BF16)
            o_ref[0, rows, lo:lo + GDN_DV] = _mm(qdec.astype(BF16), sb) + _mm(attn.astype(BF16), vnb)
            state[hh] = sh * jnp.exp(glast) + _mm_tn(kdec.astype(BF16), vnb)


def _gdn_call(q, k, v, gb, chunks):
    b, lp, w = q.shape
    tb = chunks * GDN_CHUNK
    blk = lambda width: pl.BlockSpec((1, tb, width), lambda bi, j: (bi, j, 0))
    return pl.pallas_call(
        functools.partial(_gdn_kernel, chunks=chunks),
        grid=(b, lp // tb),
        in_specs=[blk(w), blk(w), blk(w), blk(LANES)],
        out_specs=blk(w),
        out_shape=jax.ShapeDtypeStruct((b, lp, w), F32),
        scratch_shapes=[pltpu.VMEM((GDN_HEADS, GDN_DK, GDN_DV), F32)],
        compiler_params=_params("parallel", "arbitrary"),
    )(q, k, v, gb)


def _outproj_kernel(o_ref, sz_ref, yconv_ref, h_ref, wtop_ref, wbot_ref, nw_ref, g_ref, b_ref, out_ref, *, alpha):
    o = o_ref[...]
    parts = []
    for hh in range(GDN_HEADS):
        oh = o[:, hh * GDN_DV:(hh + 1) * GDN_DV]
        parts.append(oh * lax.rsqrt(jnp.mean(oh * oh, axis=-1, keepdims=True) + RMS_EPS) * nw_ref[...])
    ygdn = (jnp.concatenate(parts, axis=-1) * sz_ref[...].astype(F32)).astype(BF16)
    mix = _mm(yconv_ref[...], wtop_ref[...]) + _mm(ygdn, wbot_ref[...])
    out_ref[...] = _layer_norm(alpha * h_ref[...] + mix, g_ref[...], b_ref[...])


def _outproj_call(o, sz, yconv, h, wtop, wbot, nw, g, b, tb, alpha):
    n, d = h.shape
    full = lambda a: pl.BlockSpec(a.shape, lambda i: (0,) * a.ndim)
    blk = lambda w: pl.BlockSpec((tb, w), lambda i: (i, 0))
    return pl.pallas_call(
        functools.partial(_outproj_kernel, alpha=alpha),
        grid=(n // tb,),
        in_specs=[blk(GDN_WIDTH), blk(GDN_WIDTH), blk(CONV_WIDTH), blk(d), full(wtop), full(wbot), full(nw),
                  full(g), full(b)],
        out_specs=blk(d),
        out_shape=jax.ShapeDtypeStruct((n, d), F32),
        compiler_params=_params("parallel"),
    )(o, sz, yconv, h, wtop, wbot, nw, g, b)


def _top_values(s, n):
    tops = []
    cur = s
    for r in range(n):
        m = jnp.max(cur, axis=0, keepdims=True)
        tops.append(m)
        if r + 1 < n:
            cur = jnp.where(cur >= m, NEG_INF, cur)
    return jnp.concatenate(tops, axis=0)


def _peer_kernel(x_ref, wq_ref, k1_ref, k2_ref, u_ref, vt_ref, g_ref, b_ref, out_ref,
                 xb_ref, acc_ref, th_ref, aa_ref, s2_ref, bb_ref, *, alpha, ec):
    e = pl.program_id(1)
    tb = x_ref.shape[0]
    nk = PEER_TOPK + 1

    @pl.when(e == 0)
    def _():
        xb = x_ref[...].astype(BF16)
        xb_ref[...] = xb
        acc_ref[...] = jnp.zeros_like(acc_ref)
        q = _mm(xb, wq_ref[...]).astype(BF16)
        for hd in range(PEER_HEADS):
            lo = hd * 2 * PEER_HALF
            s1 = _mm_nt(k1_ref[...], q[:, lo:lo + PEER_HALF])
            s2 = _mm_nt(k2_ref[...], q[:, lo + PEER_HALF:lo + 2 * PEER_HALF])
            t1 = _top_values(s1, nk)
            t2 = _top_values(s2, nk)
            cand = jnp.concatenate([t1[r:r + 1] + t2[:nk // (r + 1)] for r in range(nk)], axis=0)
            best = _top_values(cand, nk)
            tau = 0.5 * (best[PEER_TOPK - 1:PEER_TOPK] + best[PEER_TOPK:PEER_TOPK + 1])
            z = jnp.sum(jnp.exp(best[:PEER_TOPK] - best[0:1]), axis=0, keepdims=True)
            th = tau - s1
            aa = jnp.exp(s1 - t1[0:1]) / z
            for grp in range(PEER_NKEYS // SUBLANES):
                th_ref[hd, grp] = th[grp * SUBLANES:(grp + 1) * SUBLANES]
                aa_ref[hd, grp] = aa[grp * SUBLANES:(grp + 1) * SUBLANES]
            s2_ref[hd] = s2
            bb_ref[hd] = jnp.exp(s2 - t2[0:1])

    ht = _mm_nt(u_ref[...], xb_ref[...])
    act = 0.5 * ht * (1.0 + lax.erf(ht * np.float32(np.sqrt(0.5))))
    rows_per = PEER_NKEYS
    assert ec == SUBLANES * rows_per
    coef = []
    for ii in range(SUBLANES):
        tiles = []
        for lt in range(tb // LANES):
            ls = slice(lt * LANES, (lt + 1) * LANES)
            gsum = jnp.zeros((rows_per, LANES), F32)
            for hd in range(PEER_HEADS):
                th = th_ref[hd, e, ii:ii + 1, ls]
                aa = aa_ref[hd, e, ii:ii + 1, ls]
                gsum = gsum + jnp.where(s2_ref[hd, :, ls] >= th, bb_ref[hd, :, ls], 0.0) * aa
            tiles.append(gsum)
        coef.append(jnp.concatenate(tiles, axis=1))
    coef = (jnp.concatenate(coef, axis=0) * act).astype(BF16)
    acc_ref[...] += _mm(vt_ref[0], coef)

    @pl.when(e == pl.num_programs(1) - 1)
    def _():
        y = acc_ref[...].T
        out_ref[...] = _layer_norm(alpha * x_ref[...] + y, g_ref[...], b_ref[...])


def _peer_call(x, wq, k1, k2, u, vt, g, b, tb, ec, alpha):
    n, d = x.shape
    ne = u.shape[0] // ec
    full = lambda a: pl.BlockSpec(a.shape, lambda i, e: (0,) * a.ndim)
    head_scratch = pltpu.VMEM((PEER_HEADS, PEER_NKEYS, tb), F32)
    grouped_scratch = pltpu.VMEM((PEER_HEADS, PEER_NKEYS // SUBLANES, SUBLANES, tb), F32)
    return pl.pallas_call(
        functools.partial(_peer_kernel, alpha=alpha, ec=ec),
        grid=(n // tb, ne),
        in_specs=[pl.BlockSpec((tb, d), lambda i, e: (i, 0)), full(wq), full(k1), full(k2),
                  pl.BlockSpec((ec, d), lambda i, e: (e, 0)),
                  pl.BlockSpec((1, d, ec), lambda i, e: (e, 0, 0)),
                  full(g), full(b)],
        out_specs=pl.BlockSpec((tb, d), lambda i, e: (i, 0)),
        out_shape=jax.ShapeDtypeStruct((n, d), F32),
        scratch_shapes=[pltpu.VMEM((tb, d), BF16), pltpu.VMEM((d, tb), F32),
                        grouped_scratch, grouped_scratch, head_scratch, head_scratch],
        compiler_params=_params("parallel", "arbitrary"),
    )(x, wq, k1, k2, u, vt, g, b)


def kernel(x, meta_tokens, ln_in_g, ln_in_b, w_in, conv_w, gdn_conv_w, a_log, dt_bias, gdn_norm_w, w_out, ln1_g,
           ln1_b, peer_w_q, peer_k1, peer_k2, peer_u, peer_v, ln2_g, ln2_b):
    bsz, seq, d = x.shape
    depth = w_in.shape[0]
    alpha = (2.0 * depth) ** 0.25
    lp = SEQ_PAD + N_META + seq
    n = bsz * lp
    assert lp % GDN_CHUNK == 0

    tb_row = _pick_block(n, 768, LANES)
    tb_in = _pick_block(lp, 768, 2 * SUBLANES)
    chunks = 3 if (lp // GDN_CHUNK) % 3 == 0 else 1
    tb_peer = _pick_block(n, 384, LANES)
    ec = SUBLANES * PEER_NKEYS
    n_exp = peer_u.shape[1]

    row = lambda a: a.reshape(1, -1).astype(F32)
    pad_lanes = lambda a: jnp.pad(a.reshape(1, -1).astype(F32), ((0, 0), (0, LANES - a.size)))

    meta = jnp.broadcast_to(meta_tokens[None].astype(x.dtype), (bsz, N_META, d))
    h = jnp.concatenate([jnp.zeros((bsz, SEQ_PAD, d), x.dtype), meta, x], axis=1).reshape(n, d)
    h = _ln_call(h, row(ln_in_g), row(ln_in_b), tb_row)

    c3 = 3 * CONV_WIDTH
    for l in range(depth):
        wl = w_in[l]
        wc = wl[:, :c3].astype(BF16)
        wqkv = wl[:, c3:c3 + QKV_COLS].astype(BF16)
        wz = wl[:, c3 + QKV_COLS:c3 + QKV_COLS + GDN_WIDTH].astype(BF16)
        wab = jnp.pad(wl[:, c3 + QKV_COLS + GDN_WIDTH:], ((0, 0), (0, LANES - 2 * GDN_HEADS))).astype(BF16)
        yconv, q, k, v, sz, gb = _inproj_call(
            h.reshape(bsz, lp, d), wc, wqkv, wz, wab, conv_w[l].astype(F32), gdn_conv_w[l].astype(F32),
            pad_lanes(a_log[l]), pad_lanes(dt_bias[l]), tb_in)
        o = _gdn_call(q, k, v, gb, chunks)
        wo = w_out[l].astype(BF16)
        h = _outproj_call(o.reshape(n, GDN_WIDTH), sz.reshape(n, GDN_WIDTH), yconv.reshape(n, CONV_WIDTH), h,
                          wo[:CONV_WIDTH], wo[CONV_WIDTH:], row(gdn_norm_w[l]), row(ln1_g[l]), row(ln1_b[l]),
                          tb_row, alpha)
        ub = peer_u[l].astype(BF16)
        vt = peer_v[l].astype(BF16).reshape(n_exp // ec, ec, d).transpose(0, 2, 1)
        h = _peer_call(h, peer_w_q[l].astype(BF16), peer_k1[l].astype(BF16), peer_k2[l].astype(BF16), ub, vt,
                       row(ln2_g[l]), row(ln2_b[l]), tb_peer, ec, alpha)
    return h.reshape(bsz, lp, d)[:, SEQ_PAD + N_META:]
```

```python
import functools

import jax
import jax.numpy as jnp
import numpy as np
from jax import lax
from jax.experimental import pallas as pl
from jax.experimental.pallas import tpu as pltpu

F32 = jnp.float32
BF16 = jnp.bfloat16

N_META = 16
CONV_WIDTH = 512
CONV_K = 3
GDN_HEADS = 4
GDN_DK = 128
GDN_DV = 128
GDN_CONV_K = 4
GDN_CHUNK = 64
QKV_COLS = GDN_HEADS * (2 * GDN_DK + GDN_DV)
GDN_WIDTH = GDN_HEADS * GDN_DV
PEER_HEADS = 8
PEER_NKEYS = 128
PEER_HALF = 128
PEER_TOPK = 16
LN_EPS = 1e-5
RMS_EPS = 1e-6
SEQ_PAD = (-N_META) % GDN_CHUNK
LANES = 128
SUBLANES = 8
VMEM_LIMIT = 56 * 1024 * 1024
NEG_INF = float("-inf")


def _mm(a, b, precision=None):
    return jnp.dot(a, b, preferred_element_type=F32, precision=precision)


def _mm_nt(a, b, precision=None):
    return lax.dot_general(a, b, (((1,), (1,)), ((), ())), preferred_element_type=F32, precision=precision)


def _mm_tn(a, b, precision=None):
    return lax.dot_general(a, b, (((0,), (0,)), ((), ())), preferred_element_type=F32, precision=precision)


def _layer_norm(x, g, b):
    mu = jnp.mean(x, axis=-1, keepdims=True)
    xc = x - mu
    var = jnp.mean(xc * xc, axis=-1, keepdims=True)
    return xc * lax.rsqrt(var + LN_EPS) * g + b


def _silu(x):
    return x * jax.nn.sigmoid(x)


def _pick_block(n, target, mult):
    best = None
    for d in range(mult, min(n, target) + 1, mult):
        if n % d == 0:
            best = d
    assert best is not None, (n, target, mult)
    return best


def _params(*sem):
    return pltpu.CompilerParams(dimension_semantics=sem, vmem_limit_bytes=VMEM_LIMIT)


def _ln_kernel(x_ref, g_ref, b_ref, o_ref):
    o_ref[...] = _layer_norm(x_ref[...], g_ref[...], b_ref[...])


def _ln_call(x2d, g, b, tb):
    n, d = x2d.shape
    return pl.pallas_call(
        _ln_kernel,
        grid=(n // tb,),
        in_specs=[pl.BlockSpec((tb, d), lambda i: (i, 0)),
                  pl.BlockSpec((1, d), lambda i: (0, 0)),
                  pl.BlockSpec((1, d), lambda i: (0, 0))],
        out_specs=pl.BlockSpec((tb, d), lambda i: (i, 0)),
        out_shape=jax.ShapeDtypeStruct((n, d), F32),
        compiler_params=_params("parallel"),
    )(x2d, g, b)


def _causal_conv(u, prev, w):
    k = w.shape[0]

    def taps(a):
        acc = a * w[k - 1:k]
        for s in range(1, k):
            acc = acc + pltpu.roll(a, s, 0) * w[k - 1 - s:k - s]
        return acc

    body = taps(u)
    head = taps(jnp.concatenate([prev, u[:SUBLANES]], axis=0))[SUBLANES:]
    return jnp.concatenate([head, body[SUBLANES:]], axis=0)


def _inproj_kernel(h_ref, wc_ref, wqkv_ref, wz_ref, wab_ref, cw_ref, gcw_ref, alog_ref, dtb_ref,
                   yconv_ref, q_ref, k_ref, v_ref, sz_ref, gb_ref, carry_c, carry_qkv):
    j = pl.program_id(1)
    tb = h_ref.shape[1]

    @pl.when(j == 0)
    def _():
        carry_c[...] = jnp.zeros_like(carry_c)
        carry_qkv[...] = jnp.zeros_like(carry_qkv)

    xb = h_ref[0].astype(BF16)
    row = j * tb + lax.broadcasted_iota(jnp.int32, (tb, 1), 0)
    valid = row >= SEQ_PAD

    p = _mm(xb, wc_ref[...])
    u = jnp.where(valid, p[:, CONV_WIDTH:2 * CONV_WIDTH] * p[:, 2 * CONV_WIDTH:], 0.0)
    yconv_ref[0] = (p[:, :CONV_WIDTH] * _causal_conv(u, carry_c[...], cw_ref[...])).astype(BF16)
    carry_c[...] = u[tb - SUBLANES:]

    p = jnp.where(valid, _mm(xb, wqkv_ref[...]), 0.0)
    s = _silu(_causal_conv(p, carry_qkv[...], gcw_ref[...]))
    carry_qkv[...] = p[tb - SUBLANES:]
    for hh in range(GDN_HEADS):
        lo = hh * GDN_DK
        qh = s[:, lo:lo + GDN_DK]
        kh = s[:, GDN_WIDTH + lo:GDN_WIDTH + lo + GDN_DK]
        q_ref[0, :, lo:lo + GDN_DK] = qh * lax.rsqrt(jnp.sum(qh * qh, axis=-1, keepdims=True) + RMS_EPS)
        k_ref[0, :, lo:lo + GDN_DK] = kh * lax.rsqrt(jnp.sum(kh * kh, axis=-1, keepdims=True) + RMS_EPS)
    v_ref[0] = s[:, 2 * GDN_WIDTH:]

    sz_ref[0] = _silu(_mm(xb, wz_ref[...])).astype(BF16)

    ab = _mm(xb, wab_ref[...])
    t = ab + dtb_ref[...]
    softplus = jnp.maximum(t, 0.0) + jnp.log1p(jnp.exp(-jnp.abs(t)))
    g = -jnp.exp(alog_ref[...]) * softplus
    lane = lax.broadcasted_iota(jnp.int32, (1, LANES), 1)
    gb = jnp.where(lane < GDN_HEADS, g, jax.nn.sigmoid(ab))
    gb_ref[0] = jnp.where(valid, gb, 0.0)


def _inproj_call(h, wc, wqkv, wz, wab, cw, gcw, alog, dtb, tb):
    b, lp, d = h.shape
    full = lambda a: pl.BlockSpec(a.shape, lambda bi, j: (0,) * a.ndim)
    blk = lambda w: pl.BlockSpec((1, tb, w), lambda bi, j: (bi, j, 0))
    out = lambda w, dt: jax.ShapeDtypeStruct((b, lp, w), dt)
    return pl.pallas_call(
        _inproj_kernel,
        grid=(b, lp // tb),
        in_specs=[blk(d), full(wc), full(wqkv), full(wz), full(wab), full(cw), full(gcw), full(alog), full(dtb)],
        out_specs=[blk(CONV_WIDTH), blk(GDN_WIDTH), blk(GDN_WIDTH), blk(GDN_WIDTH), blk(GDN_WIDTH), blk(LANES)],
        out_shape=[out(CONV_WIDTH, BF16), out(GDN_WIDTH, F32), out(GDN_WIDTH, F32), out(GDN_WIDTH, F32),
                   out(GDN_WIDTH, BF16), out(LANES, F32)],
        scratch_shapes=[pltpu.VMEM((SUBLANES, CONV_WIDTH), F32), pltpu.VMEM((SUBLANES, QKV_COLS), F32)],
        compiler_params=_params("parallel", "arbitrary"),
    )(h, wc, wqkv, wz, wab, cw, gcw, alog, dtb)


GDN_BASE = 16


def _gdn_local_kernel(q_ref, k_ref, v_ref, gb_ref, w_ref, qd_ref, kd_ref, u_ref, attn_ref, egl_ref, *, chunks):
    c = GDN_CHUNK
    hi = lax.Precision.HIGHEST
    rows_all = chunks * c

    ri = lax.broadcasted_iota(jnp.int32, (c, c), 0)
    ci = lax.broadcasted_iota(jnp.int32, (c, c), 1)
    causal = ri >= ci
    strict = ri > ci
    eye = (ri == ci).astype(F32)
    base = jnp.logical_and(strict, (ri // GDN_BASE) == (ci // GDN_BASE))
    levels = [jnp.logical_and(jnp.logical_and((ri >> (l + 1)) == (ci >> (l + 1)), ((ri >> l) & 1) == 1),
                              ((ci >> l) & 1) == 0) for l in range(GDN_BASE.bit_length() - 1, c.bit_length() - 1)]
    rr = lax.broadcasted_iota(jnp.int32, (rows_all, rows_all), 0)
    rc = lax.broadcasted_iota(jnp.int32, (rows_all, rows_all), 1)
    ltri = jnp.logical_and(rr >= rc, (rr // c) == (rc // c)).astype(F32)
    sel = (lax.broadcasted_iota(jnp.int32, (SUBLANES, LANES), 0)
           == lax.broadcasted_iota(jnp.int32, (SUBLANES, LANES), 1)).astype(F32)

    gb_all = gb_ref[...]
    gcum_all = _mm(ltri, gb_all, hi)
    gcum_t_all = _mm_nt(sel, gcum_all, hi)

    for cc in range(chunks):
        last = cc * c + c - 1
        egl_ref[cc] = jnp.exp(jnp.broadcast_to(gcum_t_all[:, last:last + 1], (SUBLANES, LANES)))

    items = [(cc, hh) for cc in range(chunks) for hh in range(GDN_HEADS)]
    each = lambda fn, *lists: [fn(*args) for args in zip(*lists)]
    rows_of = [slice(cc * c, (cc + 1) * c) for cc, _ in items]
    cols_of = [slice(hh * GDN_DK, (hh + 1) * GDN_DK) for _, hh in items]

    kh = each(lambda r, l: k_ref[r, l], rows_of, cols_of)
    khb = each(lambda x: x.astype(BF16), kh)
    beta = [gb_all[r, GDN_HEADS + hh:GDN_HEADS + hh + 1] for r, (_, hh) in zip(rows_of, items)]
    gcol = [gcum_all[r, hh:hh + 1] for r, (_, hh) in zip(rows_of, items)]
    grow = [gcum_t_all[hh:hh + 1, r] for r, (_, hh) in zip(rows_of, items)]
    decay = each(lambda gc_, gr_: jnp.where(causal, jnp.exp(jnp.where(causal, gc_ - gr_, 0.0)), 0.0), gcol, grow)
    kbeta = each(lambda x, b_: x * b_, kh, beta)
    a = each(lambda kb_, k_, d_: jnp.where(strict, _mm_nt(kb_.astype(BF16), k_) * d_, 0.0), kbeta, khb, decay)
    n1 = each(lambda a_: jnp.where(base, -a_, 0.0), a)
    n1b = each(lambda x: x.astype(BF16), n1)
    n2 = each(lambda x: _mm(x, x), n1b)
    npow = each(lambda x: x.astype(BF16), n2)
    tinv = each(lambda n1_, n2_, n1b_, n2b_: eye + n1_ + n2_ + _mm(n1b_, n2b_), n1, n2, n1b, npow)
    for _ in range(GDN_BASE.bit_length() - 3):
        npow = each(lambda x: _mm(x, x).astype(BF16), npow)
        tinv = each(lambda t_, p_: t_ + _mm(t_.astype(BF16), p_), tinv, npow)
    for m in levels:
        tb16 = each(lambda t_: t_.astype(BF16), tinv)
        mt = each(lambda a_, t_: _mm(jnp.where(m, a_, 0.0).astype(BF16), t_).astype(BF16), a, tb16)
        tinv = each(lambda t_, tb_, mt_: t_ - _mm(tb_, mt_), tinv, tb16, mt)
    tb16 = each(lambda t_: t_.astype(BF16), tinv)
    egc = each(jnp.exp, gcol)
    vh = each(lambda r, l: v_ref[r, l], rows_of, cols_of)
    rhs = each(lambda v_, b_, kb_, e_: jnp.concatenate([v_ * b_, kb_ * e_], axis=1).astype(BF16),
               vh, beta, kbeta, egc)
    uw = each(_mm, tb16, rhs)
    qh = each(lambda r, l: q_ref[r, l] * (GDN_DK ** -0.5), rows_of, cols_of)
    attn = each(lambda q_, k_, d_: jnp.where(causal, _mm_nt(q_.astype(BF16), k_) * d_, 0.0).astype(BF16),
                qh, khb, decay)
    for idx, (cc, hh) in enumerate(items):
        r, l = rows_of[idx], cols_of[idx]
        u_ref[r, l] = uw[idx][:, :GDN_DV]
        w_ref[r, l] = uw[idx][:, GDN_DV:].astype(BF16)
        attn_ref[r, hh * c:(hh + 1) * c] = attn[idx]
        glast = gcol[idx][c - 1:c, :]
        kd_ref[r, l] = (kh[idx] * jnp.exp(glast - gcol[idx])).astype(BF16)
        qd_ref[r, l] = (qh[idx] * egc[idx]).astype(BF16)


def _gdn_local_call(q, k, v, gb, chunks):
    n, w = q.shape
    tb = chunks * GDN_CHUNK
    blk = lambda width: pl.BlockSpec((tb, width), lambda i: (i, 0))
    out = lambda width, dt: jax.ShapeDtypeStruct((n, width), dt)
    return pl.pallas_call(
        functools.partial(_gdn_local_kernel, chunks=chunks),
        grid=(n // tb,),
        in_specs=[blk(w), blk(w), blk(w), blk(LANES)],
        out_specs=[blk(w), blk(w), blk(w), blk(w), blk(GDN_HEADS * GDN_CHUNK),
                   pl.BlockSpec((chunks, SUBLANES, LANES), lambda i: (i, 0, 0))],
        out_shape=[out(w, BF16), out(w, BF16), out(w, BF16), out(w, F32), out(GDN_HEADS * GDN_CHUNK, BF16),
                   jax.ShapeDtypeStruct((n // GDN_CHUNK, SUBLANES, LANES), F32)],
        compiler_params=_params("parallel"),
    )(q, k, v, gb)


def _gdn_scan_kernel(w_ref, qd_ref, kd_ref, u_ref, attn_ref, egl_ref, o_ref, state, *, chunks):
    c = GDN_CHUNK
    bsz = w_ref.shape[0]

    @pl.when(pl.program_id(0) == 0)
    def _():
        state[...] = jnp.zeros_like(state)

    items = [(b, hh) for b in range(bsz) for hh in range(GDN_HEADS)]
    each = lambda fn, *lists: [fn(*args) for args in zip(*lists)]
    cols_of = [slice(hh * GDN_DK, (hh + 1) * GDN_DK) for _, hh in items]
    sh = [state[idx] for idx in range(len(items))]
    for cc in range(chunks):
        rows = slice(cc * c, (cc + 1) * c)
        sb = each(lambda s_: s_.astype(BF16), sh)
        wq = [jnp.concatenate([w_ref[b, rows, l], qd_ref[b, rows, l]], axis=0) for (b, _), l in zip(items, cols_of)]
        r = each(_mm, wq, sb)
        vnb = [(u_ref[b, rows, l] - r_[:c]).astype(BF16) for (b, _), l, r_ in zip(items, cols_of, r)]
        for (b, hh), l, r_, v_ in zip(items, cols_of, r, vnb):
            o_ref[b, rows, l] = r_[c:] + _mm(attn_ref[b, rows, hh * c:(hh + 1) * c], v_)
        sh = [s_ * egl_ref[b, cc, hh:hh + 1, :] + _mm_tn(kd_ref[b, rows, l], v_)
              for (b, hh), l, s_, v_ in zip(items, cols_of, sh, vnb)]
    for idx, s_ in enumerate(sh):
        state[idx] = s_


def _gdn_scan_call(w, qd, kd, u, attn, egl, chunks):
    b, lp, width = u.shape
    tb = chunks * GDN_CHUNK
    blk = lambda wd: pl.BlockSpec((b, tb, wd), lambda j: (0, j, 0))
    return pl.pallas_call(
        functools.partial(_gdn_scan_kernel, chunks=chunks),
        grid=(lp // tb,),
        in_specs=[blk(width), blk(width), blk(width), blk(width), blk(GDN_HEADS * GDN_CHUNK),
                  pl.BlockSpec((b, chunks, SUBLANES, LANES), lambda j: (0, j, 0, 0))],
        out_specs=blk(width),
        out_shape=jax.ShapeDtypeStruct((b, lp, width), F32),
        scratch_shapes=[pltpu.VMEM((b * GDN_HEADS, GDN_DK, GDN_DV), F32)],
        compiler_params=_params("arbitrary"),
    )(w, qd, kd, u, attn, egl)


def _outproj_kernel(o_ref, sz_ref, yconv_ref, h_ref, wtop_ref, wbot_ref, nw_ref, g_ref, b_ref, out_ref, *, alpha):
    o = o_ref[...]
    parts = []
    for hh in range(GDN_HEADS):
        oh = o[:, hh * GDN_DV:(hh + 1) * GDN_DV]
        parts.append(oh * lax.rsqrt(jnp.mean(oh * oh, axis=-1, keepdims=True) + RMS_EPS) * nw_ref[...])
    ygdn = (jnp.concatenate(parts, axis=-1) * sz_ref[...].astype(F32)).astype(BF16)
    mix = _mm(yconv_ref[...], wtop_ref[...]) + _mm(ygdn, wbot_ref[...])
    out_ref[...] = _layer_norm(alpha * h_ref[...] + mix, g_ref[...], b_ref[...])


def _outproj_call(o, sz, yconv, h, wtop, wbot, nw, g, b, tb, alpha):
    n, d = h.shape
    full = lambda a: pl.BlockSpec(a.shape, lambda i: (0,) * a.ndim)
    blk = lambda w: pl.BlockSpec((tb, w), lambda i: (i, 0))
    return pl.pallas_call(
        functools.partial(_outproj_kernel, alpha=alpha),
        grid=(n // tb,),
        in_specs=[blk(GDN_WIDTH), blk(GDN_WIDTH), blk(CONV_WIDTH), blk(d), full(wtop), full(wbot), full(nw),
                  full(g), full(b)],
        out_specs=blk(d),
        out_shape=jax.ShapeDtypeStruct((n, d), F32),
        compiler_params=_params("parallel"),
    )(o, sz, yconv, h, wtop, wbot, nw, g, b)


def _top_values(s, n):
    tops = []
    cur = s
    for r in range(n):
        m = jnp.max(cur, axis=0, keepdims=True)
        tops.append(m)
        if r + 1 < n:
            cur = jnp.where(cur >= m, NEG_INF, cur)
    return jnp.concatenate(tops, axis=0)


def _peer_kernel(x_ref, wq_ref, k1_ref, k2_ref, u_ref, vt_ref, g_ref, b_ref, out_ref,
                 xb_ref, acc_ref, th_ref, aa_ref, s2_ref, bb_ref, *, alpha, ec):
    e = pl.program_id(1)
    tb = x_ref.shape[0]
    nk = PEER_TOPK + 1

    @pl.when(e == 0)
    def _():
        xb = x_ref[...].astype(BF16)
        xb_ref[...] = xb
        acc_ref[...] = jnp.zeros_like(acc_ref)
        q = _mm(xb, wq_ref[...]).astype(BF16)
        for hd in range(PEER_HEADS):
            lo = hd * 2 * PEER_HALF
            s1 = _mm_nt(k1_ref[...], q[:, lo:lo + PEER_HALF])
            s2 = _mm_nt(k2_ref[...], q[:, lo + PEER_HALF:lo + 2 * PEER_HALF])
            t1 = _top_values(s1, nk)
            t2 = _top_values(s2, nk)
            cand = jnp.concatenate([t1[r:r + 1] + t2[:nk // (r + 1)] for r in range(nk)], axis=0)
            best = _top_values(cand, nk)
            tau = 0.5 * (best[PEER_TOPK - 1:PEER_TOPK] + best[PEER_TOPK:PEER_TOPK + 1])
            z = jnp.sum(jnp.exp(best[:PEER_TOPK] - best[0:1]), axis=0, keepdims=True)
            th = tau - s1
            aa = jnp.exp(s1 - t1[0:1]) / z
            for grp in range(PEER_NKEYS // SUBLANES):
                th_ref[hd, grp] = th[grp * SUBLANES:(grp + 1) * SUBLANES]
                aa_ref[hd, grp] = aa[grp * SUBLANES:(grp + 1) * SUBLANES]
            s2_ref[hd] = s2
            bb_ref[hd] = jnp.exp(s2 - t2[0:1])

    ht = _mm_nt(u_ref[...], xb_ref[...])
    act = 0.5 * ht * (1.0 + lax.erf(ht * np.float32(np.sqrt(0.5))))
    rows_per = PEER_NKEYS
    assert ec == SUBLANES * rows_per
    coef = []
    for ii in range(SUBLANES):
        tiles = []
        for lt in range(tb // LANES):
            ls = slice(lt * LANES, (lt + 1) * LANES)
            gsum = jnp.zeros((rows_per, LANES), F32)
            for hd in range(PEER_HEADS):
                th = th_ref[hd, e, ii:ii + 1, ls]
                aa = aa_ref[hd, e, ii:ii + 1, ls]
                gsum = gsum + jnp.where(s2_ref[hd, :, ls] >= th, bb_ref[hd, :, ls], 0.0) * aa
            tiles.append(gsum)
        coef.append(jnp.concatenate(tiles, axis=1))
    coef = (jnp.concatenate(coef, axis=0) * act).astype(BF16)
    acc_ref[...] += _mm(vt_ref[0], coef)

    @pl.when(e == pl.num_programs(1) - 1)
    def _():
        y = acc_ref[...].T
        out_ref[...] = _layer_norm(alpha * x_ref[...] + y, g_ref[...], b_ref[...])


def _peer_call(x, wq, k1, k2, u, vt, g, b, tb, ec, alpha):
    n, d = x.shape
    ne = u.shape[0] // ec
    full = lambda a: pl.BlockSpec(a.shape, lambda i, e: (0,) * a.ndim)
    head_scratch = pltpu.VMEM((PEER_HEADS, PEER_NKEYS, tb), F32)
    grouped_scratch = pltpu.VMEM((PEER_HEADS, PEER_NKEYS // SUBLANES, SUBLANES, tb), F32)
    return pl.pallas_call(
        functools.partial(_peer_kernel, alpha=alpha, ec=ec),
        grid=(n // tb, ne),
        in_specs=[pl.BlockSpec((tb, d), lambda i, e: (i, 0)), full(wq), full(k1), full(k2),
                  pl.BlockSpec((ec, d), lambda i, e: (e, 0)),
                  pl.BlockSpec((1, d, ec), lambda i, e: (e, 0, 0)),
                  full(g), full(b)],
        out_specs=pl.BlockSpec((tb, d), lambda i, e: (i, 0)),
        out_shape=jax.ShapeDtypeStruct((n, d), F32),
        scratch_shapes=[pltpu.VMEM((tb, d), BF16), pltpu.VMEM((d, tb), F32),
                        grouped_scratch, grouped_scratch, head_scratch, head_scratch],
        compiler_params=_params("parallel", "arbitrary"),
    )(x, wq, k1, k2, u, vt, g, b)


def kernel(x, meta_tokens, ln_in_g, ln_in_b, w_in, conv_w, gdn_conv_w, a_log, dt_bias, gdn_norm_w, w_out, ln1_g,
           ln1_b, peer_w_q, peer_k1, peer_k2, peer_u, peer_v, ln2_g, ln2_b):
    bsz, seq, d = x.shape
    depth = w_in.shape[0]
    alpha = (2.0 * depth) ** 0.25
    lp = SEQ_PAD + N_META + seq
    n = bsz * lp
    assert lp % GDN_CHUNK == 0

    tb_row = _pick_block(n, 768, LANES)
    tb_in = _pick_block(lp, 768, 2 * SUBLANES)
    chunks = 3 if (lp // GDN_CHUNK) % 3 == 0 else 1
    tb_peer = _pick_block(n, 384, LANES)
    ec = SUBLANES * PEER_NKEYS
    n_exp = peer_u.shape[1]

    row = lambda a: a.reshape(1, -1).astype(F32)
    pad_lanes = lambda a: jnp.pad(a.reshape(1, -1).astype(F32), ((0, 0), (0, LANES - a.size)))

    meta = jnp.broadcast_to(meta_tokens[None].astype(x.dtype), (bsz, N_META, d))
    h = jnp.concatenate([jnp.zeros((bsz, SEQ_PAD, d), x.dtype), meta, x], axis=1).reshape(n, d)
    h = _ln_call(h, row(ln_in_g), row(ln_in_b), tb_row)

    c3 = 3 * CONV_WIDTH
    for l in range(depth):
        wl = w_in[l]
        wc = wl[:, :c3].astype(BF16)
        wqkv = wl[:, c3:c3 + QKV_COLS].astype(BF16)
        wz = wl[:, c3 + QKV_COLS:c3 + QKV_COLS + GDN_WIDTH].astype(BF16)
        wab = jnp.pad(wl[:, c3 + QKV_COLS + GDN_WIDTH:], ((0, 0), (0, LANES - 2 * GDN_HEADS))).astype(BF16)
        yconv, q, k, v, sz, gb = _inproj_call(
            h.reshape(bsz, lp, d), wc, wqkv, wz, wab, conv_w[l].astype(F32), gdn_conv_w[l].astype(F32),
            pad_lanes(a_log[l]), pad_lanes(dt_bias[l]), tb_in)
        flat = lambda a: a.reshape(n, a.shape[-1])
        wg, qd, kd, ug, attn, egl = _gdn_local_call(flat(q), flat(k), flat(v), flat(gb), chunks)
        seq = lambda a: a.reshape(bsz, lp, a.shape[-1])
        o = _gdn_scan_call(seq(wg), seq(qd), seq(kd), seq(ug), seq(attn),
                           egl.reshape(bsz, lp // GDN_CHUNK, SUBLANES, LANES), chunks)
        wo = w_out[l].astype(BF16)
        h = _outproj_call(o.reshape(n, GDN_WIDTH), sz.reshape(n, GDN_WIDTH), yconv.reshape(n, CONV_WIDTH), h,
                          wo[:CONV_WIDTH], wo[CONV_WIDTH:], row(gdn_norm_w[l]), row(ln1_g[l]), row(ln1_b[l]),
                          tb_row, alpha)
        ub = peer_u[l].astype(BF16)
        vt = peer_v[l].astype(BF16).reshape(n_exp // ec, ec, d).transpose(0, 2, 1)
        h = _peer_call(h, peer_w_q[l].astype(BF16), peer_k1[l].astype(BF16), peer_k2[l].astype(BF16), ub, vt,
                       row(ln2_g[l]), row(ln2_b[l]), tb_peer, ec, alpha)
    return h.reshape(bsz, lp, d)[:, SEQ_PAD + N_META:]
```

```python
import functools

import jax
import jax.numpy as jnp
import numpy as np
from jax import lax
from jax.experimental import pallas as pl
from jax.experimental.pallas import tpu as pltpu

F32 = jnp.float32
BF16 = jnp.bfloat16

N_META = 16
CONV_WIDTH = 512
CONV_K = 3
GDN_HEADS = 4
GDN_DK = 128
GDN_DV = 128
GDN_CONV_K = 4
GDN_CHUNK = 64
QKV_COLS = GDN_HEADS * (2 * GDN_DK + GDN_DV)
GDN_WIDTH = GDN_HEADS * GDN_DV
PEER_HEADS = 8
PEER_NKEYS = 128
PEER_HALF = 128
PEER_TOPK = 16
LN_EPS = 1e-5
RMS_EPS = 1e-6
SEQ_PAD = (-N_META) % GDN_CHUNK
LANES = 128
SUBLANES = 8
VMEM_LIMIT = 56 * 1024 * 1024
NEG_INF = float("-inf")


def _mm(a, b, precision=None):
    return jnp.dot(a, b, preferred_element_type=F32, precision=precision)


def _mm_nt(a, b, precision=None):
    return lax.dot_general(a, b, (((1,), (1,)), ((), ())), preferred_element_type=F32, precision=precision)


def _mm_tn(a, b, precision=None):
    return lax.dot_general(a, b, (((0,), (0,)), ((), ())), preferred_element_type=F32, precision=precision)


def _layer_norm(x, g, b):
    mu = jnp.mean(x, axis=-1, keepdims=True)
    xc = x - mu
    var = jnp.mean(xc * xc, axis=-1, keepdims=True)
    return xc * lax.rsqrt(var + LN_EPS) * g + b


def _silu(x):
    return x * jax.nn.sigmoid(x)


def _pick_block(n, target, mult):
    best = None
    for d in range(mult, min(n, target) + 1, mult):
        if n % d == 0:
            best = d
    assert best is not None, (n, target, mult)
    return best


def _params(*sem):
    return pltpu.CompilerParams(dimension_semantics=sem, vmem_limit_bytes=VMEM_LIMIT)


def _ln_kernel(x_ref, g_ref, b_ref, o_ref):
    o_ref[...] = _layer_norm(x_ref[...], g_ref[...], b_ref[...])


def _ln_call(x2d, g, b, tb):
    n, d = x2d.shape
    return pl.pallas_call(
        _ln_kernel,
        grid=(n // tb,),
        in_specs=[pl.BlockSpec((tb, d), lambda i: (i, 0)),
                  pl.BlockSpec((1, d), lambda i: (0, 0)),
                  pl.BlockSpec((1, d), lambda i: (0, 0))],
        out_specs=pl.BlockSpec((tb, d), lambda i: (i, 0)),
        out_shape=jax.ShapeDtypeStruct((n, d), F32),
        compiler_params=_params("parallel"),
    )(x2d, g, b)


def _causal_conv(u, prev, w):
    k = w.shape[0]

    def taps(a):
        acc = a * w[k - 1:k]
        for s in range(1, k):
            acc = acc + pltpu.roll(a, s, 0) * w[k - 1 - s:k - s]
        return acc

    body = taps(u)
    head = taps(jnp.concatenate([prev, u[:SUBLANES]], axis=0))[SUBLANES:]
    return jnp.concatenate([head, body[SUBLANES:]], axis=0)


def _inproj_kernel(h_ref, wc_ref, wqkv_ref, wz_ref, wab_ref, cw_ref, gcw_ref, alog_ref, dtb_ref,
                   yconv_ref, q_ref, k_ref, v_ref, sz_ref, gb_ref, carry_c, carry_qkv):
    j = pl.program_id(1)
    tb = h_ref.shape[1]

    @pl.when(j == 0)
    def _():
        carry_c[...] = jnp.zeros_like(carry_c)
        carry_qkv[...] = jnp.zeros_like(carry_qkv)

    xb = h_ref[0].astype(BF16)
    row = j * tb + lax.broadcasted_iota(jnp.int32, (tb, 1), 0)
    valid = row >= SEQ_PAD

    p = _mm(xb, wc_ref[...])
    u = jnp.where(valid, p[:, CONV_WIDTH:2 * CONV_WIDTH] * p[:, 2 * CONV_WIDTH:], 0.0)
    yconv_ref[0] = (p[:, :CONV_WIDTH] * _causal_conv(u, carry_c[...], cw_ref[...])).astype(BF16)
    carry_c[...] = u[tb - SUBLANES:]

    p = jnp.where(valid, _mm(xb, wqkv_ref[...]), 0.0)
    s = _silu(_causal_conv(p, carry_qkv[...], gcw_ref[...]))
    carry_qkv[...] = p[tb - SUBLANES:]
    for hh in range(GDN_HEADS):
        lo = hh * GDN_DK
        qh = s[:, lo:lo + GDN_DK]
        kh = s[:, GDN_WIDTH + lo:GDN_WIDTH + lo + GDN_DK]
        q_ref[0, :, lo:lo + GDN_DK] = qh * lax.rsqrt(jnp.sum(qh * qh, axis=-1, keepdims=True) + RMS_EPS)
        k_ref[0, :, lo:lo + GDN_DK] = kh * lax.rsqrt(jnp.sum(kh * kh, axis=-1, keepdims=True) + RMS_EPS)
    v_ref[0] = s[:, 2 * GDN_WIDTH:]

    sz_ref[0] = _silu(_mm(xb, wz_ref[...])).astype(BF16)

    ab = _mm(xb, wab_ref[...])
    t = ab + dtb_ref[...]
    softplus = jnp.maximum(t, 0.0) + jnp.log1p(jnp.exp(-jnp.abs(t)))
    g = -jnp.exp(alog_ref[...]) * softplus
    lane = lax.broadcasted_iota(jnp.int32, (1, LANES), 1)
    gb = jnp.where(lane < GDN_HEADS, g, jax.nn.sigmoid(ab))
    gb_ref[0] = jnp.where(valid, gb, 0.0)


def _inproj_call(h, wc, wqkv, wz, wab, cw, gcw, alog, dtb, tb):
    b, lp, d = h.shape
    full = lambda a: pl.BlockSpec(a.shape, lambda bi, j: (0,) * a.ndim)
    blk = lambda w: pl.BlockSpec((1, tb, w), lambda bi, j: (bi, j, 0))
    out = lambda w, dt: jax.ShapeDtypeStruct((b, lp, w), dt)
    return pl.pallas_call(
        _inproj_kernel,
        grid=(b, lp // tb),
        in_specs=[blk(d), full(wc), full(wqkv), full(wz), full(wab), full(cw), full(gcw), full(alog), full(dtb)],
        out_specs=[blk(CONV_WIDTH), blk(GDN_WIDTH), blk(GDN_WIDTH), blk(GDN_WIDTH), blk(GDN_WIDTH), blk(LANES)],
        out_shape=[out(CONV_WIDTH, BF16), out(GDN_WIDTH, F32), out(GDN_WIDTH, F32), out(GDN_WIDTH, F32),
                   out(GDN_WIDTH, BF16), out(LANES, F32)],
        scratch_shapes=[pltpu.VMEM((SUBLANES, CONV_WIDTH), F32), pltpu.VMEM((SUBLANES, QKV_COLS), F32)],
        compiler_params=_params("parallel", "arbitrary"),
    )(h, wc, wqkv, wz, wab, cw, gcw, alog, dtb)


GDN_BASE = 16


def _gdn_local_kernel(q_ref, k_ref, v_ref, gb_ref, w_ref, qd_ref, kd_ref, u_ref, attn_ref, egl_ref, *, chunks):
    c = GDN_CHUNK
    hi = lax.Precision.HIGHEST
    rows_all = chunks * c

    ri = lax.broadcasted_iota(jnp.int32, (c, c), 0)
    ci = lax.broadcasted_iota(jnp.int32, (c, c), 1)
    causal = ri >= ci
    strict = ri > ci
    eye = (ri == ci).astype(F32)
    base = jnp.logical_and(strict, (ri // GDN_BASE) == (ci // GDN_BASE))
    levels = [jnp.logical_and(jnp.logical_and((ri >> (l + 1)) == (ci >> (l + 1)), ((ri >> l) & 1) == 1),
                              ((ci >> l) & 1) == 0) for l in range(GDN_BASE.bit_length() - 1, c.bit_length() - 1)]
    rr = lax.broadcasted_iota(jnp.int32, (rows_all, rows_all), 0)
    rc = lax.broadcasted_iota(jnp.int32, (rows_all, rows_all), 1)
    ltri = jnp.logical_and(rr >= rc, (rr // c) == (rc // c)).astype(F32)
    sel = (lax.broadcasted_iota(jnp.int32, (SUBLANES, LANES), 0)
           == lax.broadcasted_iota(jnp.int32, (SUBLANES, LANES), 1)).astype(F32)

    gb_all = gb_ref[...]
    gcum_all = _mm(ltri, gb_all, hi)
    gcum_t_all = _mm_nt(sel, gcum_all, hi)

    for cc in range(chunks):
        last = cc * c + c - 1
        egl_ref[cc] = jnp.exp(jnp.broadcast_to(gcum_t_all[:, last:last + 1], (SUBLANES, LANES)))

    items = [(cc, hh) for cc in range(chunks) for hh in range(GDN_HEADS)]
    each = lambda fn, *lists: [fn(*args) for args in zip(*lists)]
    rows_of = [slice(cc * c, (cc + 1) * c) for cc, _ in items]
    cols_of = [slice(hh * GDN_DK, (hh + 1) * GDN_DK) for _, hh in items]

    kh = each(lambda r, l: k_ref[r, l], rows_of, cols_of)
    khb = each(lambda x: x.astype(BF16), kh)
    beta = [gb_all[r, GDN_HEADS + hh:GDN_HEADS + hh + 1] for r, (_, hh) in zip(rows_of, items)]
    gcol = [gcum_all[r, hh:hh + 1] for r, (_, hh) in zip(rows_of, items)]
    grow = [gcum_t_all[hh:hh + 1, r] for r, (_, hh) in zip(rows_of, items)]
    decay = each(lambda gc_, gr_: jnp.where(causal, jnp.exp(jnp.where(causal, gc_ - gr_, 0.0)), 0.0), gcol, grow)
    kbeta = each(lambda x, b_: x * b_, kh, beta)
    a = each(lambda kb_, k_, d_: jnp.where(strict, _mm_nt(kb_.astype(BF16), k_) * d_, 0.0), kbeta, khb, decay)
    n1 = each(lambda a_: jnp.where(base, -a_, 0.0), a)
    n1b = each(lambda x: x.astype(BF16), n1)
    n2 = each(lambda x: _mm(x, x), n1b)
    npow = each(lambda x: x.astype(BF16), n2)
    tinv = each(lambda n1_, n2_, n1b_, n2b_: eye + n1_ + n2_ + _mm(n1b_, n2b_), n1, n2, n1b, npow)
    for _ in range(GDN_BASE.bit_length() - 3):
        npow = each(lambda x: _mm(x, x).astype(BF16), npow)
        tinv = each(lambda t_, p_: t_ + _mm(t_.astype(BF16), p_), tinv, npow)
    for m in levels:
        tb16 = each(lambda t_: t_.astype(BF16), tinv)
        mt = each(lambda a_, t_: _mm(jnp.where(m, a_, 0.0).astype(BF16), t_).astype(BF16), a, tb16)
        tinv = each(lambda t_, tb_, mt_: t_ - _mm(tb_, mt_), tinv, tb16, mt)
    tb16 = each(lambda t_: t_.astype(BF16), tinv)
    egc = each(jnp.exp, gcol)
    vh = each(lambda r, l: v_ref[r, l], rows_of, cols_of)
    rhs = each(lambda v_, b_, kb_, e_: jnp.concatenate([v_ * b_, kb_ * e_], axis=1).astype(BF16),
               vh, beta, kbeta, egc)
    uw = each(_mm, tb16, rhs)
    qh = each(lambda r, l: q_ref[r, l] * (GDN_DK ** -0.5), rows_of, cols_of)
    attn = each(lambda q_, k_, d_: jnp.where(causal, _mm_nt(q_.astype(BF16), k_) * d_, 0.0).astype(BF16),
                qh, khb, decay)
    for idx, (cc, hh) in enumerate(items):
        r, l = rows_of[idx], cols_of[idx]
        u_ref[r, l] = uw[idx][:, :GDN_DV]
        w_ref[r, l] = uw[idx][:, GDN_DV:].astype(BF16)
        attn_ref[r, hh * c:(hh + 1) * c] = attn[idx]
        glast = gcol[idx][c - 1:c, :]
        kd_ref[r, l] = (kh[idx] * jnp.exp(glast - gcol[idx])).astype(BF16)
        qd_ref[r, l] = (qh[idx] * egc[idx]).astype(BF16)


def _gdn_local_call(q, k, v, gb, chunks):
    n, w = q.shape
    tb = chunks * GDN_CHUNK
    blk = lambda width: pl.BlockSpec((tb, width), lambda i: (i, 0))
    out = lambda width, dt: jax.ShapeDtypeStruct((n, width), dt)
    return pl.pallas_call(
        functools.partial(_gdn_local_kernel, chunks=chunks),
        grid=(n // tb,),
        in_specs=[blk(w), blk(w), blk(w), blk(LANES)],
        out_specs=[blk(w), blk(w), blk(w), blk(w), blk(GDN_HEADS * GDN_CHUNK),
                   pl.BlockSpec((chunks, SUBLANES, LANES), lambda i: (i, 0, 0))],
        out_shape=[out(w, BF16), out(w, BF16), out(w, BF16), out(w, F32), out(GDN_HEADS * GDN_CHUNK, BF16),
                   jax.ShapeDtypeStruct((n // GDN_CHUNK, SUBLANES, LANES), F32)],
        compiler_params=_params("parallel"),
    )(q, k, v, gb)


def _gdn_scan_kernel(w_ref, qd_ref, kd_ref, u_ref, attn_ref, egl_ref, o_ref, state, *, chunks):
    c = GDN_CHUNK
    bsz = w_ref.shape[0]

    @pl.when(pl.program_id(0) == 0)
    def _():
        state[...] = jnp.zeros_like(state)

    items = [(b, hh) for b in range(bsz) for hh in range(GDN_HEADS)]
    each = lambda fn, *lists: [fn(*args) for args in zip(*lists)]
    cols_of = [slice(hh * GDN_DK, (hh + 1) * GDN_DK) for _, hh in items]
    sh = [state[idx] for idx in range(len(items))]
    for cc in range(chunks):
        rows = slice(cc * c, (cc + 1) * c)
        sb = each(lambda s_: s_.astype(BF16), sh)
        wq = [jnp.concatenate([w_ref[b, rows, l], qd_ref[b, rows, l]], axis=0) for (b, _), l in zip(items, cols_of)]
        r = each(_mm, wq, sb)
        vnb = [(u_ref[b, rows, l] - r_[:c]).astype(BF16) for (b, _), l, r_ in zip(items, cols_of, r)]
        for (b, hh), l, r_, v_ in zip(items, cols_of, r, vnb):
            o_ref[b, rows, l] = r_[c:] + _mm(attn_ref[b, rows, hh * c:(hh + 1) * c], v_)
        sh = [s_ * egl_ref[b, cc, hh:hh + 1, :] + _mm_tn(kd_ref[b, rows, l], v_)
              for (b, hh), l, s_, v_ in zip(items, cols_of, sh, vnb)]
    for idx, s_ in enumerate(sh):
        state[idx] = s_


def _gdn_scan_call(w, qd, kd, u, attn, egl, chunks):
    b, lp, width = u.shape
    tb = chunks * GDN_CHUNK
    blk = lambda wd: pl.BlockSpec((b, tb, wd), lambda j: (0, j, 0))
    return pl.pallas_call(
        functools.partial(_gdn_scan_kernel, chunks=chunks),
        grid=(lp // tb,),
        in_specs=[blk(width), blk(width), blk(width), blk(width), blk(GDN_HEADS * GDN_CHUNK),
                  pl.BlockSpec((b, chunks, SUBLANES, LANES), lambda j: (0, j, 0, 0))],
        out_specs=blk(width),
        out_shape=jax.ShapeDtypeStruct((b, lp, width), F32),
        scratch_shapes=[pltpu.VMEM((b * GDN_HEADS, GDN_DK, GDN_DV), F32)],
        compiler_params=_params("arbitrary"),
    )(w, qd, kd, u, attn, egl)


def _outproj_kernel(o_ref, sz_ref, yconv_ref, h_ref, wtop_ref, wbot_ref, nw_ref, g_ref, b_ref, out_ref, *, alpha):
    o = o_ref[...]
    parts = []
    for hh in range(GDN_HEADS):
        oh = o[:, hh * GDN_DV:(hh + 1) * GDN_DV]
        parts.append(oh * lax.rsqrt(jnp.mean(oh * oh, axis=-1, keepdims=True) + RMS_EPS) * nw_ref[...])
    ygdn = (jnp.concatenate(parts, axis=-1) * sz_ref[...].astype(F32)).astype(BF16)
    mix = _mm(yconv_ref[...], wtop_ref[...]) + _mm(ygdn, wbot_ref[...])
    out_ref[...] = _layer_norm(alpha * h_ref[...] + mix, g_ref[...], b_ref[...])


def _outproj_call(o, sz, yconv, h, wtop, wbot, nw, g, b, tb, alpha):
    n, d = h.shape
    full = lambda a: pl.BlockSpec(a.shape, lambda i: (0,) * a.ndim)
    blk = lambda w: pl.BlockSpec((tb, w), lambda i: (i, 0))
    return pl.pallas_call(
        functools.partial(_outproj_kernel, alpha=alpha),
        grid=(n // tb,),
        in_specs=[blk(GDN_WIDTH), blk(GDN_WIDTH), blk(CONV_WIDTH), blk(d), full(wtop), full(wbot), full(nw),
                  full(g), full(b)],
        out_specs=blk(d),
        out_shape=jax.ShapeDtypeStruct((n, d), F32),
        compiler_params=_params("parallel"),
    )(o, sz, yconv, h, wtop, wbot, nw, g, b)


def _top_values(s, n):
    tops = []
    cur = s
    for r in range(n):
        m = jnp.max(cur, axis=0, keepdims=True)
        tops.append(m)
        if r + 1 < n:
            cur = jnp.where(cur >= m, NEG_INF, cur)
    return jnp.concatenate(tops, axis=0)


def _peer_kernel(xa_ref, xc_ref, wq_ref, k1_ref, k2_ref, u_ref, vt_ref, g_ref, b_ref, out_ref,
                 xb_ref, acc_ref, ht_ref, coef_ref, th_ref, aa_ref, s2_ref, bb_ref, *, alpha, ec, ne, total):
    s = pl.program_id(0)
    tb = xa_ref.shape[0]
    nk = PEER_TOPK + 1
    chunk_a = jnp.clip(s, 0, total - 1) % ne
    chunk_b = (s - 1) % ne
    chunk_c = (s - 2) % ne

    @pl.when(s == 0)
    def _():
        ht_ref[...] = jnp.zeros_like(ht_ref)
        coef_ref[...] = jnp.zeros_like(coef_ref)
        th_ref[...] = jnp.zeros_like(th_ref)
        aa_ref[...] = jnp.zeros_like(aa_ref)
        s2_ref[...] = jnp.zeros_like(s2_ref)
        bb_ref[...] = jnp.zeros_like(bb_ref)
        acc_ref[...] = jnp.zeros_like(acc_ref)

    @pl.when(jnp.logical_and(s < total, chunk_a == 0))
    def _():
        xb_ref[...] = xa_ref[...].astype(BF16)

    @pl.when(jnp.logical_and(jnp.logical_and(s >= 1, s <= total), chunk_b == 0))
    def _():
        xb = xb_ref[...]
        for hd in range(PEER_HEADS):
            lo = hd * 2 * PEER_HALF
            q = _mm(xb, wq_ref[:, lo:lo + 2 * PEER_HALF]).astype(BF16)
            s1 = _mm_nt(k1_ref[...], q[:, :PEER_HALF])
            s2 = _mm_nt(k2_ref[...], q[:, PEER_HALF:])
            t1 = _top_values(s1, nk)
            t2 = _top_values(s2, nk)
            cand = jnp.concatenate([t1[r:r + 1] + t2[:nk // (r + 1)] for r in range(nk)], axis=0)
            best = _top_values(cand, nk)
            tau = 0.5 * (best[PEER_TOPK - 1:PEER_TOPK] + best[PEER_TOPK:PEER_TOPK + 1])
            z = jnp.sum(jnp.exp(best[:PEER_TOPK] - best[0:1]), axis=0, keepdims=True)
            th = tau - s1
            aa = jnp.exp(s1 - t1[0:1]) / z
            for grp in range(PEER_NKEYS // SUBLANES):
                th_ref[hd, grp] = th[grp * SUBLANES:(grp + 1) * SUBLANES]
                aa_ref[hd, grp] = aa[grp * SUBLANES:(grp + 1) * SUBLANES]
            s2_ref[hd] = s2
            bb_ref[hd] = jnp.exp(s2 - t2[0:1])

    @pl.when(jnp.logical_and(s >= 2, chunk_c == 0))
    def _():
        acc_ref[...] = jnp.zeros_like(acc_ref)

    slot_a = s % 2
    slot_b = (s + 1) % 2
    rows_per = PEER_NKEYS
    assert ec == SUBLANES * rows_per

    def stage_a(ts, ms):
        ht_ref[slot_a, ms, ts] = _mm_nt(u_ref[ms, :], xb_ref[ts, :])

    def stage_b(ig, jh, lt):
        ls = slice(lt * LANES, (lt + 1) * LANES)
        js = slice(jh * (rows_per // 2), (jh + 1) * (rows_per // 2))
        iis = range(ig * PEER_IGROUP, (ig + 1) * PEER_IGROUP)
        gsum = [jnp.zeros((rows_per // 2, LANES), F32) for _ in iis]
        for hd in range(PEER_HEADS):
            s2 = s2_ref[hd, js, ls]
            bb = bb_ref[hd, js, ls]
            for k, ii in enumerate(iis):
                th = th_ref[hd, chunk_b, ii:ii + 1, ls]
                aa = aa_ref[hd, chunk_b, ii:ii + 1, ls]
                gsum[k] = gsum[k] + jnp.where(s2 >= th, bb, 0.0) * aa
        for k, ii in enumerate(iis):
            rs = slice(ii * rows_per + js.start, ii * rows_per + js.stop)
            ht = ht_ref[slot_b, rs, ls]
            act = 0.5 * ht * (1.0 + lax.erf(ht * np.float32(np.sqrt(0.5))))
            coef_ref[slot_b, rs, ls] = (gsum[k] * act).astype(BF16)

    def stage_c(ts, ms):
        acc_ref[ms, ts] += _mm(vt_ref[0, ms, :], coef_ref[slot_a, :, ts])

    assert acc_ref.shape[0] == ec
    mxu_pieces = [(slice(t0, t0 + PEER_SLICE), slice(m0, m0 + PEER_SLICE))
                  for t0 in range(0, tb, PEER_SLICE) for m0 in range(0, ec, PEER_SLICE)]
    valu_pieces = [(ig, jh, lt) for lt in range(tb // LANES) for ig in range(SUBLANES // PEER_IGROUP)
                   for jh in range(2)]
    per = -(-len(valu_pieces) // len(mxu_pieces))
    for idx, (ts, ms) in enumerate(mxu_pieces):
        stage_a(ts, ms)
        stage_c(ts, ms)
        for piece in valu_pieces[idx * per:(idx + 1) * per]:
            stage_b(*piece)

    @pl.when(jnp.logical_and(s >= 2, chunk_c == ne - 1))
    def _():
        y = acc_ref[...].T
        out_ref[...] = _layer_norm(alpha * xc_ref[...] + y, g_ref[...], b_ref[...])


PEER_SLICE = 256
PEER_IGROUP = 4


def _peer_call(x, wq, k1, k2, u, vt, g, b, tb, ec, alpha):
    n, d = x.shape
    ne = u.shape[0] // ec
    total = (n // tb) * ne
    once = pl.Buffered(1)
    full = lambda a: pl.BlockSpec(a.shape, lambda s: (0,) * a.ndim, pipeline_mode=once)
    blk_a = lambda s: jnp.clip(s, 0, total - 1) // ne
    blk_c = lambda s: jnp.clip(s - 2, 0, total - 1) // ne
    head_scratch = pltpu.VMEM((PEER_HEADS, PEER_NKEYS, tb), F32)
    grouped_scratch = pltpu.VMEM((PEER_HEADS, PEER_NKEYS // SUBLANES, SUBLANES, tb), F32)
    return pl.pallas_call(
        functools.partial(_peer_kernel, alpha=alpha, ec=ec, ne=ne, total=total),
        grid=(total + 2,),
        in_specs=[pl.BlockSpec((tb, d), lambda s: (blk_a(s), 0), pipeline_mode=once),
                  pl.BlockSpec((tb, d), lambda s: (blk_c(s), 0), pipeline_mode=once),
                  full(wq), full(k1), full(k2),
                  pl.BlockSpec((ec, d), lambda s: (jnp.clip(s, 0, total - 1) % ne, 0)),
                  pl.BlockSpec((1, d, ec), lambda s: (jnp.clip(s - 2, 0, total - 1) % ne, 0, 0)),
                  full(g), full(b)],
        out_specs=pl.BlockSpec((tb, d), lambda s: (blk_c(s), 0)),
        out_shape=jax.ShapeDtypeStruct((n, d), F32),
        scratch_shapes=[pltpu.VMEM((tb, d), BF16), pltpu.VMEM((d, tb), F32),
                        pltpu.VMEM((2, ec, tb), F32), pltpu.VMEM((2, ec, tb), BF16),
                        grouped_scratch, grouped_scratch, head_scratch, head_scratch],
        compiler_params=_params("arbitrary"),
    )(x, x, wq, k1, k2, u, vt, g, b)


def kernel(x, meta_tokens, ln_in_g, ln_in_b, w_in, conv_w, gdn_conv_w, a_log, dt_bias, gdn_norm_w, w_out, ln1_g,
           ln1_b, peer_w_q, peer_k1, peer_k2, peer_u, peer_v, ln2_g, ln2_b):
    bsz, seq, d = x.shape
    depth = w_in.shape[0]
    alpha = (2.0 * depth) ** 0.25
    lp = SEQ_PAD + N_META + seq
    n = bsz * lp
    assert lp % GDN_CHUNK == 0

    tb_row = _pick_block(n, 768, LANES)
    tb_in = _pick_block(lp, 768, 2 * SUBLANES)
    chunks = 3 if (lp // GDN_CHUNK) % 3 == 0 else 1
    tb_peer = _pick_block(n, 768, PEER_SLICE)
    ec = SUBLANES * PEER_NKEYS
    n_exp = peer_u.shape[1]

    row = lambda a: a.reshape(1, -1).astype(F32)
    pad_lanes = lambda a: jnp.pad(a.reshape(1, -1).astype(F32), ((0, 0), (0, LANES - a.size)))

    meta = jnp.broadcast_to(meta_tokens[None].astype(x.dtype), (bsz, N_META, d))
    h = jnp.concatenate([jnp.zeros((bsz, SEQ_PAD, d), x.dtype), meta, x], axis=1).reshape(n, d)
    h = _ln_call(h, row(ln_in_g), row(ln_in_b), tb_row)

    c3 = 3 * CONV_WIDTH
    for l in range(depth):
        wl = w_in[l]
        wc = wl[:, :c3].astype(BF16)
        wqkv = wl[:, c3:c3 + QKV_COLS].astype(BF16)
        wz = wl[:, c3 + QKV_COLS:c3 + QKV_COLS + GDN_WIDTH].astype(BF16)
        wab = jnp.pad(wl[:, c3 + QKV_COLS + GDN_WIDTH:], ((0, 0), (0, LANES - 2 * GDN_HEADS))).astype(BF16)
        yconv, q, k, v, sz, gb = _inproj_call(
            h.reshape(bsz, lp, d), wc, wqkv, wz, wab, conv_w[l].astype(F32), gdn_conv_w[l].astype(F32),
            pad_lanes(a_log[l]), pad_lanes(dt_bias[l]), tb_in)
        flat = lambda a: a.reshape(n, a.shape[-1])
        wg, qd, kd, ug, attn, egl = _gdn_local_call(flat(q), flat(k), flat(v), flat(gb), chunks)
        seq = lambda a: a.reshape(bsz, lp, a.shape[-1])
        o = _gdn_scan_call(seq(wg), seq(qd), seq(kd), seq(ug), seq(attn),
                           egl.reshape(bsz, lp // GDN_CHUNK, SUBLANES, LANES), chunks)
        wo = w_out[l].astype(BF16)
        h = _outproj_call(o.reshape(n, GDN_WIDTH), sz.reshape(n, GDN_WIDTH), yconv.reshape(n, CONV_WIDTH), h,
                          wo[:CONV_WIDTH], wo[CONV_WIDTH:], row(gdn_norm_w[l]), row(ln1_g[l]), row(ln1_b[l]),
                          tb_row, alpha)
        ub = peer_u[l].astype(BF16)
        vt = peer_v[l].astype(BF16).reshape(n_exp // ec, ec, d).transpose(0, 2, 1)
        h = _peer_call(h, peer_w_q[l].astype(BF16), peer_k1[l].astype(BF16), peer_k2[l].astype(BF16), ub, vt,
                       row(ln2_g[l]), row(ln2_b[l]), tb_peer, ec, alpha)
    return h.reshape(bsz, lp, d)[:, SEQ_PAD + N_META:]
```

```python
import functools

import jax
import jax.numpy as jnp
import numpy as np
from jax import lax
from jax.experimental import pallas as pl
from jax.experimental.pallas import tpu as pltpu

F32 = jnp.float32
BF16 = jnp.bfloat16

N_META = 16
CONV_WIDTH = 512
CONV_K = 3
GDN_HEADS = 4
GDN_DK = 128
GDN_DV = 128
GDN_CONV_K = 4
GDN_CHUNK = 64
QKV_COLS = GDN_HEADS * (2 * GDN_DK + GDN_DV)
GDN_WIDTH = GDN_HEADS * GDN_DV
PEER_HEADS = 8
PEER_NKEYS = 128
PEER_HALF = 128
PEER_TOPK = 16
LN_EPS = 1e-5
RMS_EPS = 1e-6
SEQ_PAD = (-N_META) % GDN_CHUNK
LANES = 128
SUBLANES = 8
PACK = 16
VMEM_LIMIT = 56 * 1024 * 1024
NEG_INF = float("-inf")


def _mm(a, b, precision=None):
    return jnp.dot(a, b, preferred_element_type=F32, precision=precision)


def _mm_nt(a, b, precision=None):
    return lax.dot_general(a, b, (((1,), (1,)), ((), ())), preferred_element_type=F32, precision=precision)


def _mm_tn(a, b, precision=None):
    return lax.dot_general(a, b, (((0,), (0,)), ((), ())), preferred_element_type=F32, precision=precision)


def _layer_norm(x, g, b):
    mu = jnp.mean(x, axis=-1, keepdims=True)
    xc = x - mu
    var = jnp.mean(xc * xc, axis=-1, keepdims=True)
    return xc * lax.rsqrt(var + LN_EPS) * g + b


def _silu(x):
    return x * jax.nn.sigmoid(x)


def _pick_block(n, target, mult):
    best = None
    for d in range(mult, min(n, target) + 1, mult):
        if n % d == 0:
            best = d
    assert best is not None, (n, target, mult)
    return best


def _params(*sem):
    return pltpu.CompilerParams(dimension_semantics=sem, vmem_limit_bytes=VMEM_LIMIT)


def _ln_kernel(x_ref, g_ref, b_ref, o_ref):
    o_ref[...] = _layer_norm(x_ref[...], g_ref[...], b_ref[...])


def _ln_call(x2d, g, b, tb):
    n, d = x2d.shape
    return pl.pallas_call(
        _ln_kernel,
        grid=(n // tb,),
        in_specs=[pl.BlockSpec((tb, d), lambda i: (i, 0)),
                  pl.BlockSpec((1, d), lambda i: (0, 0)),
                  pl.BlockSpec((1, d), lambda i: (0, 0))],
        out_specs=pl.BlockSpec((tb, d), lambda i: (i, 0)),
        out_shape=jax.ShapeDtypeStruct((n, d), F32),
        compiler_params=_params("parallel"),
    )(x2d, g, b)


def _causal_conv(u, prev, w):
    k = w.shape[0]

    def taps(a):
        acc = a * w[k - 1:k]
        for s in range(1, k):
            acc = acc + pltpu.roll(a, s, 0) * w[k - 1 - s:k - s]
        return acc

    body = taps(u)
    head = taps(jnp.concatenate([prev, u[:SUBLANES]], axis=0))[SUBLANES:]
    return jnp.concatenate([head, body[SUBLANES:]], axis=0)


def _inproj_kernel(h_ref, wc_ref, wqkv_ref, wz_ref, wab_ref, cw_ref, gcw_ref, alog_ref, dtb_ref,
                   yconv_ref, q_ref, k_ref, v_ref, sz_ref, gb_ref, carry_c, carry_qkv):
    j = pl.program_id(1)
    tb = h_ref.shape[1]

    @pl.when(j == 0)
    def _():
        carry_c[...] = jnp.zeros_like(carry_c)
        carry_qkv[...] = jnp.zeros_like(carry_qkv)

    xb = h_ref[0].astype(BF16)
    row = j * tb + lax.broadcasted_iota(jnp.int32, (tb, 1), 0)
    valid = row >= SEQ_PAD

    p = _mm(xb, wc_ref[...])
    u = jnp.where(valid, p[:, CONV_WIDTH:2 * CONV_WIDTH] * p[:, 2 * CONV_WIDTH:], 0.0)
    yconv_ref[0] = (p[:, :CONV_WIDTH] * _causal_conv(u, carry_c[...], cw_ref[...])).astype(BF16)
    carry_c[...] = u[tb - SUBLANES:]

    p = jnp.where(valid, _mm(xb, wqkv_ref[...]), 0.0)
    s = _silu(_causal_conv(p, carry_qkv[...], gcw_ref[...]))
    carry_qkv[...] = p[tb - SUBLANES:]
    for hh in range(GDN_HEADS):
        lo = hh * GDN_DK
        qh = s[:, lo:lo + GDN_DK]
        kh = s[:, GDN_WIDTH + lo:GDN_WIDTH + lo + GDN_DK]
        q_ref[0, :, lo:lo + GDN_DK] = qh * lax.rsqrt(jnp.sum(qh * qh, axis=-1, keepdims=True) + RMS_EPS)
        k_ref[0, :, lo:lo + GDN_DK] = kh * lax.rsqrt(jnp.sum(kh * kh, axis=-1, keepdims=True) + RMS_EPS)
    v_ref[0] = s[:, 2 * GDN_WIDTH:]

    sz_ref[0] = _silu(_mm(xb, wz_ref[...])).astype(BF16)

    ab = _mm(xb, wab_ref[...])
    t = ab + dtb_ref[...]
    softplus = jnp.maximum(t, 0.0) + jnp.log1p(jnp.exp(-jnp.abs(t)))
    g = -jnp.exp(alog_ref[...]) * softplus
    lane = lax.broadcasted_iota(jnp.int32, (1, LANES), 1)
    gb = jnp.where(lane < GDN_HEADS, g, jax.nn.sigmoid(ab))
    gb_ref[0] = jnp.where(valid, gb, 0.0)


def _inproj_call(h, wc, wqkv, wz, wab, cw, gcw, alog, dtb, tb):
    b, lp, d = h.shape
    full = lambda a: pl.BlockSpec(a.shape, lambda bi, j: (0,) * a.ndim)
    blk = lambda w: pl.BlockSpec((1, tb, w), lambda bi, j: (bi, j, 0))
    out = lambda w, dt: jax.ShapeDtypeStruct((b, lp, w), dt)
    return pl.pallas_call(
        _inproj_kernel,
        grid=(b, lp // tb),
        in_specs=[blk(d), full(wc), full(wqkv), full(wz), full(wab), full(cw), full(gcw), full(alog), full(dtb)],
        out_specs=[blk(CONV_WIDTH), blk(GDN_WIDTH), blk(GDN_WIDTH), blk(GDN_WIDTH), blk(GDN_WIDTH), blk(LANES)],
        out_shape=[out(CONV_WIDTH, BF16), out(GDN_WIDTH, F32), out(GDN_WIDTH, F32), out(GDN_WIDTH, F32),
                   out(GDN_WIDTH, BF16), out(LANES, F32)],
        scratch_shapes=[pltpu.VMEM((SUBLANES, CONV_WIDTH), F32), pltpu.VMEM((SUBLANES, QKV_COLS), F32)],
        compiler_params=_params("parallel", "arbitrary"),
    )(h, wc, wqkv, wz, wab, cw, gcw, alog, dtb)


GDN_BASE = 16


def _gdn_local_kernel(q_ref, k_ref, v_ref, gb_ref, w_ref, qd_ref, kd_ref, u_ref, attn_ref, egl_ref, *, chunks):
    c = GDN_CHUNK
    hi = lax.Precision.HIGHEST
    rows_all = chunks * c

    ri = lax.broadcasted_iota(jnp.int32, (c, c), 0)
    ci = lax.broadcasted_iota(jnp.int32, (c, c), 1)
    causal = ri >= ci
    strict = ri > ci
    eye = (ri == ci).astype(F32)
    base = jnp.logical_and(strict, (ri // GDN_BASE) == (ci // GDN_BASE))
    levels = [jnp.logical_and(jnp.logical_and((ri >> (l + 1)) == (ci >> (l + 1)), ((ri >> l) & 1) == 1),
                              ((ci >> l) & 1) == 0) for l in range(GDN_BASE.bit_length() - 1, c.bit_length() - 1)]
    rr = lax.broadcasted_iota(jnp.int32, (rows_all, rows_all), 0)
    rc = lax.broadcasted_iota(jnp.int32, (rows_all, rows_all), 1)
    ltri = jnp.logical_and(rr >= rc, (rr // c) == (rc // c)).astype(F32)
    sel = (lax.broadcasted_iota(jnp.int32, (SUBLANES, LANES), 0)
           == lax.broadcasted_iota(jnp.int32, (SUBLANES, LANES), 1)).astype(F32)

    gb_all = gb_ref[...]
    gcum_all = _mm(ltri, gb_all, hi)
    gcum_t_all = _mm_nt(sel, gcum_all, hi)

    for cc in range(chunks):
        last = cc * c + c - 1
        egl_ref[cc] = jnp.exp(jnp.broadcast_to(gcum_t_all[:, last:last + 1], (SUBLANES, LANES)))

    items = [(cc, hh) for cc in range(chunks) for hh in range(GDN_HEADS)]
    each = lambda fn, *lists: [fn(*args) for args in zip(*lists)]
    rows_of = [slice(cc * c, (cc + 1) * c) for cc, _ in items]
    cols_of = [slice(hh * GDN_DK, (hh + 1) * GDN_DK) for _, hh in items]

    kh = each(lambda r, l: k_ref[r, l], rows_of, cols_of)
    khb = each(lambda x: x.astype(BF16), kh)
    beta = [gb_all[r, GDN_HEADS + hh:GDN_HEADS + hh + 1] for r, (_, hh) in zip(rows_of, items)]
    gcol = [gcum_all[r, hh:hh + 1] for r, (_, hh) in zip(rows_of, items)]
    grow = [gcum_t_all[hh:hh + 1, r] for r, (_, hh) in zip(rows_of, items)]
    decay = each(lambda gc_, gr_: jnp.where(causal, jnp.exp(jnp.where(causal, gc_ - gr_, 0.0)), 0.0), gcol, grow)
    kbeta = each(lambda x, b_: x * b_, kh, beta)
    a = each(lambda kb_, k_, d_: jnp.where(strict, _mm_nt(kb_.astype(BF16), k_) * d_, 0.0), kbeta, khb, decay)
    n1 = each(lambda a_: jnp.where(base, -a_, 0.0), a)
    n1b = each(lambda x: x.astype(BF16), n1)
    n2 = each(lambda x: _mm(x, x), n1b)
    npow = each(lambda x: x.astype(BF16), n2)
    tinv = each(lambda n1_, n2_, n1b_, n2b_: eye + n1_ + n2_ + _mm(n1b_, n2b_), n1, n2, n1b, npow)
    for _ in range(GDN_BASE.bit_length() - 3):
        npow = each(lambda x: _mm(x, x).astype(BF16), npow)
        tinv = each(lambda t_, p_: t_ + _mm(t_.astype(BF16), p_), tinv, npow)
    for m in levels:
        tb16 = each(lambda t_: t_.astype(BF16), tinv)
        mt = each(lambda a_, t_: _mm(jnp.where(m, a_, 0.0).astype(BF16), t_).astype(BF16), a, tb16)
        tinv = each(lambda t_, tb_, mt_: t_ - _mm(tb_, mt_), tinv, tb16, mt)
    tb16 = each(lambda t_: t_.astype(BF16), tinv)
    egc = each(jnp.exp, gcol)
    vh = each(lambda r, l: v_ref[r, l], rows_of, cols_of)
    rhs = each(lambda v_, b_, kb_, e_: jnp.concatenate([v_ * b_, kb_ * e_], axis=1).astype(BF16),
               vh, beta, kbeta, egc)
    uw = each(_mm, tb16, rhs)
    qh = each(lambda r, l: q_ref[r, l] * (GDN_DK ** -0.5), rows_of, cols_of)
    attn = each(lambda q_, k_, d_: jnp.where(causal, _mm_nt(q_.astype(BF16), k_) * d_, 0.0).astype(BF16),
                qh, khb, decay)
    for idx, (cc, hh) in enumerate(items):
        r, l = rows_of[idx], cols_of[idx]
        u_ref[r, l] = uw[idx][:, :GDN_DV]
        w_ref[r, l] = uw[idx][:, GDN_DV:].astype(BF16)
        attn_ref[r, hh * c:(hh + 1) * c] = attn[idx]
        glast = gcol[idx][c - 1:c, :]
        kd_ref[r, l] = (kh[idx] * jnp.exp(glast - gcol[idx])).astype(BF16)
        qd_ref[r, l] = (qh[idx] * egc[idx]).astype(BF16)


def _gdn_local_call(q, k, v, gb, chunks):
    n, w = q.shape
    tb = chunks * GDN_CHUNK
    blk = lambda width: pl.BlockSpec((tb, width), lambda i: (i, 0))
    out = lambda width, dt: jax.ShapeDtypeStruct((n, width), dt)
    return pl.pallas_call(
        functools.partial(_gdn_local_kernel, chunks=chunks),
        grid=(n // tb,),
        in_specs=[blk(w), blk(w), blk(w), blk(LANES)],
        out_specs=[blk(w), blk(w), blk(w), blk(w), blk(GDN_HEADS * GDN_CHUNK),
                   pl.BlockSpec((chunks, SUBLANES, LANES), lambda i: (i, 0, 0))],
        out_shape=[out(w, BF16), out(w, BF16), out(w, BF16), out(w, F32), out(GDN_HEADS * GDN_CHUNK, BF16),
                   jax.ShapeDtypeStruct((n // GDN_CHUNK, SUBLANES, LANES), F32)],
        compiler_params=_params("parallel"),
    )(q, k, v, gb)


def _gdn_scan_kernel(w_ref, qd_ref, kd_ref, u_ref, attn_ref, egl_ref, o_ref, state, *, chunks):
    c = GDN_CHUNK
    bsz = w_ref.shape[0]

    @pl.when(pl.program_id(0) == 0)
    def _():
        state[...] = jnp.zeros_like(state)

    items = [(b, hh) for b in range(bsz) for hh in range(GDN_HEADS)]
    each = lambda fn, *lists: [fn(*args) for args in zip(*lists)]
    cols_of = [slice(hh * GDN_DK, (hh + 1) * GDN_DK) for _, hh in items]
    sh = [state[idx] for idx in range(len(items))]
    for cc in range(chunks):
        rows = slice(cc * c, (cc + 1) * c)
        sb = each(lambda s_: s_.astype(BF16), sh)
        wq = [jnp.concatenate([w_ref[b, rows, l], qd_ref[b, rows, l]], axis=0) for (b, _), l in zip(items, cols_of)]
        r = each(_mm, wq, sb)
        vnb = [(u_ref[b, rows, l] - r_[:c]).astype(BF16) for (b, _), l, r_ in zip(items, cols_of, r)]
        for (b, hh), l, r_, v_ in zip(items, cols_of, r, vnb):
            o_ref[b, rows, l] = r_[c:] + _mm(attn_ref[b, rows, hh * c:(hh + 1) * c], v_)
        sh = [s_ * egl_ref[b, cc, hh:hh + 1, :] + _mm_tn(kd_ref[b, rows, l], v_)
              for (b, hh), l, s_, v_ in zip(items, cols_of, sh, vnb)]
    for idx, s_ in enumerate(sh):
        state[idx] = s_


def _gdn_scan_call(w, qd, kd, u, attn, egl, chunks):
    b, lp, width = u.shape
    tb = chunks * GDN_CHUNK
    blk = lambda wd: pl.BlockSpec((b, tb, wd), lambda j: (0, j, 0))
    return pl.pallas_call(
        functools.partial(_gdn_scan_kernel, chunks=chunks),
        grid=(lp // tb,),
        in_specs=[blk(width), blk(width), blk(width), blk(width), blk(GDN_HEADS * GDN_CHUNK),
                  pl.BlockSpec((b, chunks, SUBLANES, LANES), lambda j: (0, j, 0, 0))],
        out_specs=blk(width),
        out_shape=jax.ShapeDtypeStruct((b, lp, width), F32),
        scratch_shapes=[pltpu.VMEM((b * GDN_HEADS, GDN_DK, GDN_DV), F32)],
        compiler_params=_params("arbitrary"),
    )(w, qd, kd, u, attn, egl)


def _outproj_kernel(o_ref, sz_ref, yconv_ref, h_ref, wtop_ref, wbot_ref, nw_ref, g_ref, b_ref, out_ref, *, alpha):
    o = o_ref[...]
    parts = []
    for hh in range(GDN_HEADS):
        oh = o[:, hh * GDN_DV:(hh + 1) * GDN_DV]
        parts.append(oh * lax.rsqrt(jnp.mean(oh * oh, axis=-1, keepdims=True) + RMS_EPS) * nw_ref[...])
    ygdn = (jnp.concatenate(parts, axis=-1) * sz_ref[...].astype(F32)).astype(BF16)
    mix = _mm(yconv_ref[...], wtop_ref[...]) + _mm(ygdn, wbot_ref[...])
    out_ref[...] = _layer_norm(alpha * h_ref[...] + mix, g_ref[...], b_ref[...])


def _outproj_call(o, sz, yconv, h, wtop, wbot, nw, g, b, tb, alpha):
    n, d = h.shape
    full = lambda a: pl.BlockSpec(a.shape, lambda i: (0,) * a.ndim)
    blk = lambda w: pl.BlockSpec((tb, w), lambda i: (i, 0))
    return pl.pallas_call(
        functools.partial(_outproj_kernel, alpha=alpha),
        grid=(n // tb,),
        in_specs=[blk(GDN_WIDTH), blk(GDN_WIDTH), blk(CONV_WIDTH), blk(d), full(wtop), full(wbot), full(nw),
                  full(g), full(b)],
        out_specs=blk(d),
        out_shape=jax.ShapeDtypeStruct((n, d), F32),
        compiler_params=_params("parallel"),
    )(o, sz, yconv, h, wtop, wbot, nw, g, b)


def _top_values(s, n, with_rank=False):
    tops = []
    cur = s
    rank = jnp.full(s.shape, float(n + 1), F32) if with_rank else None
    for r in range(n):
        m = jnp.max(cur, axis=0, keepdims=True)
        tops.append(m)
        hit = cur >= m
        if with_rank:
            rank = jnp.where(hit, float(r + 1), rank)
        if r + 1 < n:
            cur = jnp.where(hit, NEG_INF, cur)
    tops = jnp.concatenate(tops, axis=0)
    return (tops, rank) if with_rank else tops


def _peer_kernel(xa_ref, xc_ref, wq_ref, k1_ref, k2_ref, u_ref, vt_ref, g_ref, b_ref, out_ref,
                 xb_ref, acc_ref, ht_ref, coef_ref, np_ref, aa_ref, r2_ref, bb_ref, *, alpha, ec, ne, total):
    s = pl.program_id(0)
    tb = xa_ref.shape[0]
    nk = PEER_TOPK + 1
    chunk_a = jnp.clip(s, 0, total - 1) % ne
    chunk_b = (s - 1) % ne
    chunk_c = (s - 2) % ne

    @pl.when(s == 0)
    def _():
        ht_ref[...] = jnp.zeros_like(ht_ref)
        coef_ref[...] = jnp.zeros_like(coef_ref)
        np_ref[...] = jnp.zeros_like(np_ref)
        aa_ref[...] = jnp.zeros_like(aa_ref)
        r2_ref[...] = jnp.zeros_like(r2_ref)
        bb_ref[...] = jnp.zeros_like(bb_ref)
        acc_ref[...] = jnp.zeros_like(acc_ref)

    @pl.when(jnp.logical_and(s < total, chunk_a == 0))
    def _():
        xb_ref[...] = xa_ref[...].astype(BF16)

    @pl.when(jnp.logical_and(jnp.logical_and(s >= 1, s <= total), chunk_b == 0))
    def _():
        xb = xb_ref[...]
        for hd in range(PEER_HEADS):
            lo = hd * 2 * PEER_HALF
            q = _mm(xb, wq_ref[:, lo:lo + 2 * PEER_HALF]).astype(BF16)
            s1_all = _mm_nt(k1_ref[...], q[:, :PEER_HALF])
            s2_all = _mm_nt(k2_ref[...], q[:, PEER_HALF:])
            for lt in range(tb // LANES):
                ls = slice(lt * LANES, (lt + 1) * LANES)
                s1 = s1_all[:, ls]
                s2 = s2_all[:, ls]
                t1 = _top_values(s1, nk)
                t2, rank2 = _top_values(s2, nk, with_rank=True)
                cand = jnp.concatenate([t1[r:r + 1] + t2[:nk // (r + 1)] for r in range(nk)], axis=0)
                best = _top_values(cand, nk)
                tau = 0.5 * (best[PEER_TOPK - 1:PEER_TOPK] + best[PEER_TOPK:PEER_TOPK + 1])
                z = jnp.sum(jnp.exp(best[:PEER_TOPK] - best[0:1]), axis=0, keepdims=True)
                npair = jnp.zeros_like(s1)
                for r in range(PEER_TOPK):
                    cnt = jnp.sum((t1[r:r + 1] + t2 >= tau).astype(F32), axis=0, keepdims=True)
                    npair = jnp.where(s1 == t1[r:r + 1], cnt, npair)
                aa = jnp.exp(s1 - t1[0:1]) / z
                for grp in range(PEER_NKEYS // SUBLANES):
                    np_ref[hd, grp, :, ls] = npair[grp * SUBLANES:(grp + 1) * SUBLANES]
                    aa_ref[hd, grp, :, ls] = aa[grp * SUBLANES:(grp + 1) * SUBLANES]
                r2_ref[hd, :, ls] = rank2.astype(BF16)
                bb_ref[hd, :, ls] = jnp.exp(s2 - t2[0:1]).astype(BF16)

    @pl.when(jnp.logical_and(s >= 2, chunk_c == 0))
    def _():
        acc_ref[...] = jnp.zeros_like(acc_ref)

    slot_a = s % 2
    slot_b = (s + 1) % 2
    rows_per = PEER_NKEYS
    assert ec == SUBLANES * rows_per

    def stage_a(ts, ms):
        ht_ref[slot_a, ms, ts] = _mm_nt(u_ref[ms, :], xb_ref[ts, :])

    def stage_b(ig, jh, lt):
        ls = slice(lt * LANES, (lt + 1) * LANES)
        j0 = jh * (rows_per // PEER_JSPLIT)
        groups = [slice(j0 + g * PACK, j0 + (g + 1) * PACK) for g in range(rows_per // PEER_JSPLIT // PACK)]
        iis = range(ig * PEER_IGROUP, (ig + 1) * PEER_IGROUP)
        gsum = [[jnp.zeros((PACK, LANES), BF16) for _ in groups] for _ in iis]
        for hd in range(PEER_HEADS):
            r2 = [r2_ref[hd, js, ls] for js in groups]
            bb = [bb_ref[hd, js, ls] for js in groups]
            for k, ii in enumerate(iis):
                npair = jnp.broadcast_to(np_ref[hd, chunk_b, ii:ii + 1, ls], (PACK, LANES)).astype(BF16)
                aa = jnp.broadcast_to(aa_ref[hd, chunk_b, ii:ii + 1, ls], (PACK, LANES)).astype(BF16)
                for g in range(len(groups)):
                    gsum[k][g] = gsum[k][g] + jnp.where(r2[g] <= npair, bb[g], jnp.zeros_like(bb[g])) * aa
        for k, ii in enumerate(iis):
            for g, js in enumerate(groups):
                rs = slice(ii * rows_per + js.start, ii * rows_per + js.stop)
                ht = ht_ref[slot_b, rs, ls]
                act = 0.5 * ht * (1.0 + lax.erf(ht * np.float32(np.sqrt(0.5))))
                coef_ref[slot_b, rs, ls] = gsum[k][g] * act.astype(BF16)

    def stage_c(ts, ms):
        acc_ref[ms, ts] += _mm(vt_ref[0, ms, :], coef_ref[slot_a, :, ts])

    assert acc_ref.shape[0] == ec
    mxu_pieces = [(slice(t0, t0 + PEER_SLICE), slice(m0, m0 + PEER_MROWS))
                  for t0 in range(0, tb, PEER_SLICE) for m0 in range(0, ec, PEER_MROWS)]
    valu_pieces = [(ig, jh, lt) for lt in range(tb // LANES) for ig in range(SUBLANES // PEER_IGROUP)
                   for jh in range(PEER_JSPLIT)]
    per = -(-len(valu_pieces) // len(mxu_pieces))
    for idx, (ts, ms) in enumerate(mxu_pieces):
        stage_a(ts, ms)
        stage_c(ts, ms)
        for piece in valu_pieces[idx * per:(idx + 1) * per]:
            stage_b(*piece)

    @pl.when(jnp.logical_and(s >= 2, chunk_c == ne - 1))
    def _():
        y = acc_ref[...].T
        out_ref[...] = _layer_norm(alpha * xc_ref[...] + y, g_ref[...], b_ref[...])


PEER_SLICE = 256
PEER_MROWS = 256
PEER_IGROUP = 2
PEER_JSPLIT = 1


def _peer_call(x, wq, k1, k2, u, vt, g, b, tb, ec, alpha):
    n, d = x.shape
    ne = u.shape[0] // ec
    total = (n // tb) * ne
    once = pl.Buffered(1)
    full = lambda a: pl.BlockSpec(a.shape, lambda s: (0,) * a.ndim, pipeline_mode=once)
    blk_a = lambda s: jnp.clip(s, 0, total - 1) // ne
    blk_c = lambda s: jnp.clip(s - 2, 0, total - 1) // ne
    head_scratch = pltpu.VMEM((PEER_HEADS, PEER_NKEYS, tb), BF16)
    grouped_scratch = pltpu.VMEM((PEER_HEADS, PEER_NKEYS // SUBLANES, SUBLANES, tb), F32)
    return pl.pallas_call(
        functools.partial(_peer_kernel, alpha=alpha, ec=ec, ne=ne, total=total),
        grid=(total + 2,),
        in_specs=[pl.BlockSpec((tb, d), lambda s: (blk_a(s), 0), pipeline_mode=once),
                  pl.BlockSpec((tb, d), lambda s: (blk_c(s), 0), pipeline_mode=once),
                  full(wq), full(k1), full(k2),
                  pl.BlockSpec((ec, d), lambda s: (jnp.clip(s, 0, total - 1) % ne, 0)),
                  pl.BlockSpec((1, d, ec), lambda s: (jnp.clip(s - 2, 0, total - 1) % ne, 0, 0)),
                  full(g), full(b)],
        out_specs=pl.BlockSpec((tb, d), lambda s: (blk_c(s), 0)),
        out_shape=jax.ShapeDtypeStruct((n, d), F32),
        scratch_shapes=[pltpu.VMEM((tb, d), BF16), pltpu.VMEM((d, tb), F32),
                        pltpu.VMEM((2, ec, tb), F32), pltpu.VMEM((2, ec, tb), BF16),
                        grouped_scratch, grouped_scratch, head_scratch, head_scratch],
        compiler_params=_params("arbitrary"),
    )(x, x, wq, k1, k2, u, vt, g, b)


def kernel(x, meta_tokens, ln_in_g, ln_in_b, w_in, conv_w, gdn_conv_w, a_log, dt_bias, gdn_norm_w, w_out, ln1_g,
           ln1_b, peer_w_q, peer_k1, peer_k2, peer_u, peer_v, ln2_g, ln2_b):
    bsz, seq, d = x.shape
    depth = w_in.shape[0]
    alpha = (2.0 * depth) ** 0.25
    lp = SEQ_PAD + N_META + seq
    n = bsz * lp
    assert lp % GDN_CHUNK == 0

    tb_row = _pick_block(n, 768, LANES)
    tb_in = _pick_block(lp, 768, 2 * SUBLANES)
    chunks = 3 if (lp // GDN_CHUNK) % 3 == 0 else 1
    tb_peer = _pick_block(n, 768, PEER_SLICE)
    ec = SUBLANES * PEER_NKEYS
    n_exp = peer_u.shape[1]

    row = lambda a: a.reshape(1, -1).astype(F32)
    pad_lanes = lambda a: jnp.pad(a.reshape(1, -1).astype(F32), ((0, 0), (0, LANES - a.size)))

    meta = jnp.broadcast_to(meta_tokens[None].astype(x.dtype), (bsz, N_META, d))
    h = jnp.concatenate([jnp.zeros((bsz, SEQ_PAD, d), x.dtype), meta, x], axis=1).reshape(n, d)
    h = _ln_call(h, row(ln_in_g), row(ln_in_b), tb_row)

    c3 = 3 * CONV_WIDTH
    for l in range(depth):
        wl = w_in[l]
        wc = wl[:, :c3].astype(BF16)
        wqkv = wl[:, c3:c3 + QKV_COLS].astype(BF16)
        wz = wl[:, c3 + QKV_COLS:c3 + QKV_COLS + GDN_WIDTH].astype(BF16)
        wab = jnp.pad(wl[:, c3 + QKV_COLS + GDN_WIDTH:], ((0, 0), (0, LANES - 2 * GDN_HEADS))).astype(BF16)
        yconv, q, k, v, sz, gb = _inproj_call(
            h.reshape(bsz, lp, d), wc, wqkv, wz, wab, conv_w[l].astype(F32), gdn_conv_w[l].astype(F32),
            pad_lanes(a_log[l]), pad_lanes(dt_bias[l]), tb_in)
        flat = lambda a: a.reshape(n, a.shape[-1])
        wg, qd, kd, ug, attn, egl = _gdn_local_call(flat(q), flat(k), flat(v), flat(gb), chunks)
        seq = lambda a: a.reshape(bsz, lp, a.shape[-1])
        o = _gdn_scan_call(seq(wg), seq(qd), seq(kd), seq(ug), seq(attn),
                           egl.reshape(bsz, lp // GDN_CHUNK, SUBLANES, LANES), chunks)
        wo = w_out[l].astype(BF16)
        h = _outproj_call(o.reshape(n, GDN_WIDTH), sz.reshape(n, GDN_WIDTH), yconv.reshape(n, CONV_WIDTH), h,
                          wo[:CONV_WIDTH], wo[CONV_WIDTH:], row(gdn_norm_w[l]), row(ln1_g[l]), row(ln1_b[l]),
                          tb_row, alpha)
        ub = peer_u[l].astype(BF16)
        vt = peer_v[l].astype(BF16).reshape(n_exp // ec, ec, d).transpose(0, 2, 1)
        h = _peer_call(h, peer_w_q[l].astype(BF16), peer_k1[l].astype(BF16), peer_k2[l].astype(BF16), ub, vt,
                       row(ln2_g[l]), row(ln2_b[l]), tb_peer, ec, alpha)
    return h.reshape(bsz, lp, d)[:, SEQ_PAD + N_META:]
```

```python
import functools

import jax
import jax.numpy as jnp
import numpy as np
from jax import lax
from jax.experimental import pallas as pl
from jax.experimental.pallas import tpu as pltpu

F32 = jnp.float32
BF16 = jnp.bfloat16

N_META = 16
CONV_WIDTH = 512
CONV_K = 3
GDN_HEADS = 4
GDN_DK = 128
GDN_DV = 128
GDN_CONV_K = 4
GDN_CHUNK = 64
QKV_COLS = GDN_HEADS * (2 * GDN_DK + GDN_DV)
GDN_WIDTH = GDN_HEADS * GDN_DV
PEER_HEADS = 8
PEER_NKEYS = 128
PEER_HALF = 128
PEER_TOPK = 16
LN_EPS = 1e-5
RMS_EPS = 1e-6
SEQ_PAD = (-N_META) % GDN_CHUNK
LANES = 128
SUBLANES = 8
PACK = 16
VMEM_LIMIT = 56 * 1024 * 1024
NEG_INF = float("-inf")


def _mm(a, b, precision=None):
    return jnp.dot(a, b, preferred_element_type=F32, precision=precision)


def _mm_nt(a, b, precision=None):
    return lax.dot_general(a, b, (((1,), (1,)), ((), ())), preferred_element_type=F32, precision=precision)


def _mm_tn(a, b, precision=None):
    return lax.dot_general(a, b, (((0,), (0,)), ((), ())), preferred_element_type=F32, precision=precision)


def _layer_norm(x, g, b):
    mu = jnp.mean(x, axis=-1, keepdims=True)
    xc = x - mu
    var = jnp.mean(xc * xc, axis=-1, keepdims=True)
    return xc * lax.rsqrt(var + LN_EPS) * g + b


def _silu(x):
    return x * jax.nn.sigmoid(x)


def _pick_block(n, target, mult):
    best = None
    for d in range(mult, min(n, target) + 1, mult):
        if n % d == 0:
            best = d
    assert best is not None, (n, target, mult)
    return best


def _params(*sem):
    return pltpu.CompilerParams(dimension_semantics=sem, vmem_limit_bytes=VMEM_LIMIT)


def _ln_kernel(x_ref, g_ref, b_ref, o_ref):
    o_ref[...] = _layer_norm(x_ref[...], g_ref[...], b_ref[...])


def _ln_call(x2d, g, b, tb):
    n, d = x2d.shape
    return pl.pallas_call(
        _ln_kernel,
        grid=(n // tb,),
        in_specs=[pl.BlockSpec((tb, d), lambda i: (i, 0)),
                  pl.BlockSpec((1, d), lambda i: (0, 0)),
                  pl.BlockSpec((1, d), lambda i: (0, 0))],
        out_specs=pl.BlockSpec((tb, d), lambda i: (i, 0)),
        out_shape=jax.ShapeDtypeStruct((n, d), F32),
        compiler_params=_params("parallel"),
    )(x2d, g, b)


def _causal_conv(u, prev, w):
    k = w.shape[0]

    def taps(a):
        acc = a * w[k - 1:k]
        for s in range(1, k):
            acc = acc + pltpu.roll(a, s, 0) * w[k - 1 - s:k - s]
        return acc

    body = taps(u)
    head = taps(jnp.concatenate([prev, u[:SUBLANES]], axis=0))[SUBLANES:]
    return jnp.concatenate([head, body[SUBLANES:]], axis=0)


def _inproj_kernel(h_ref, wc_ref, wqkv_ref, wz_ref, wab_ref, cw_ref, gcw_ref, alog_ref, dtb_ref,
                   yconv_ref, q_ref, k_ref, v_ref, sz_ref, gb_ref, carry_c, carry_qkv):
    j = pl.program_id(1)
    tb = h_ref.shape[1]

    @pl.when(j == 0)
    def _():
        carry_c[...] = jnp.zeros_like(carry_c)
        carry_qkv[...] = jnp.zeros_like(carry_qkv)

    xb = h_ref[0].astype(BF16)
    row = j * tb + lax.broadcasted_iota(jnp.int32, (tb, 1), 0)
    valid = row >= SEQ_PAD

    p = _mm(xb, wc_ref[...])
    u = jnp.where(valid, p[:, CONV_WIDTH:2 * CONV_WIDTH] * p[:, 2 * CONV_WIDTH:], 0.0)
    yconv_ref[0] = (p[:, :CONV_WIDTH] * _causal_conv(u, carry_c[...], cw_ref[...])).astype(BF16)
    carry_c[...] = u[tb - SUBLANES:]

    p = jnp.where(valid, _mm(xb, wqkv_ref[...]), 0.0)
    s = _silu(_causal_conv(p, carry_qkv[...], gcw_ref[...]))
    carry_qkv[...] = p[tb - SUBLANES:]
    for hh in range(GDN_HEADS):
        lo = hh * GDN_DK
        qh = s[:, lo:lo + GDN_DK]
        kh = s[:, GDN_WIDTH + lo:GDN_WIDTH + lo + GDN_DK]
        q_ref[0, :, lo:lo + GDN_DK] = qh * lax.rsqrt(jnp.sum(qh * qh, axis=-1, keepdims=True) + RMS_EPS)
        k_ref[0, :, lo:lo + GDN_DK] = kh * lax.rsqrt(jnp.sum(kh * kh, axis=-1, keepdims=True) + RMS_EPS)
    v_ref[0] = s[:, 2 * GDN_WIDTH:]

    sz_ref[0] = _silu(_mm(xb, wz_ref[...])).astype(BF16)

    ab = _mm(xb, wab_ref[...])
    t = ab + dtb_ref[...]
    softplus = jnp.maximum(t, 0.0) + jnp.log1p(jnp.exp(-jnp.abs(t)))
    g = -jnp.exp(alog_ref[...]) * softplus
    lane = lax.broadcasted_iota(jnp.int32, (1, LANES), 1)
    gb = jnp.where(lane < GDN_HEADS, g, jax.nn.sigmoid(ab))
    gb_ref[0] = jnp.where(valid, gb, 0.0)


def _inproj_call(h, wc, wqkv, wz, wab, cw, gcw, alog, dtb, tb):
    b, lp, d = h.shape
    full = lambda a: pl.BlockSpec(a.shape, lambda bi, j: (0,) * a.ndim)
    blk = lambda w: pl.BlockSpec((1, tb, w), lambda bi, j: (bi, j, 0))
    out = lambda w, dt: jax.ShapeDtypeStruct((b, lp, w), dt)
    return pl.pallas_call(
        _inproj_kernel,
        grid=(b, lp // tb),
        in_specs=[blk(d), full(wc), full(wqkv), full(wz), full(wab), full(cw), full(gcw), full(alog), full(dtb)],
        out_specs=[blk(CONV_WIDTH), blk(GDN_WIDTH), blk(GDN_WIDTH), blk(GDN_WIDTH), blk(GDN_WIDTH), blk(LANES)],
        out_shape=[out(CONV_WIDTH, BF16), out(GDN_WIDTH, F32), out(GDN_WIDTH, F32), out(GDN_WIDTH, F32),
                   out(GDN_WIDTH, BF16), out(LANES, F32)],
        scratch_shapes=[pltpu.VMEM((SUBLANES, CONV_WIDTH), F32), pltpu.VMEM((SUBLANES, QKV_COLS), F32)],
        compiler_params=_params("parallel", "arbitrary"),
    )(h, wc, wqkv, wz, wab, cw, gcw, alog, dtb)


GDN_BASE = 16


def _gdn_local_kernel(q_ref, k_ref, v_ref, gb_ref, w_ref, qd_ref, kd_ref, u_ref, attn_ref, egl_ref, *, chunks):
    c = GDN_CHUNK
    hi = lax.Precision.HIGHEST
    rows_all = chunks * c

    ri = lax.broadcasted_iota(jnp.int32, (c, c), 0)
    ci = lax.broadcasted_iota(jnp.int32, (c, c), 1)
    causal = ri >= ci
    strict = ri > ci
    eye = (ri == ci).astype(F32)
    base = jnp.logical_and(strict, (ri // GDN_BASE) == (ci // GDN_BASE))
    levels = [jnp.logical_and(jnp.logical_and((ri >> (l + 1)) == (ci >> (l + 1)), ((ri >> l) & 1) == 1),
                              ((ci >> l) & 1) == 0) for l in range(GDN_BASE.bit_length() - 1, c.bit_length() - 1)]
    rr = lax.broadcasted_iota(jnp.int32, (rows_all, rows_all), 0)
    rc = lax.broadcasted_iota(jnp.int32, (rows_all, rows_all), 1)
    ltri = jnp.logical_and(rr >= rc, (rr // c) == (rc // c)).astype(F32)
    sel = (lax.broadcasted_iota(jnp.int32, (SUBLANES, LANES), 0)
           == lax.broadcasted_iota(jnp.int32, (SUBLANES, LANES), 1)).astype(F32)

    gb_all = gb_ref[...]
    gcum_all = _mm(ltri, gb_all, hi)
    gcum_t_all = _mm_nt(sel, gcum_all, hi)

    for cc in range(chunks):
        last = cc * c + c - 1
        egl_ref[cc] = jnp.exp(jnp.broadcast_to(gcum_t_all[:, last:last + 1], (SUBLANES, LANES)))

    items = [(cc, hh) for cc in range(chunks) for hh in range(GDN_HEADS)]
    each = lambda fn, *lists: [fn(*args) for args in zip(*lists)]
    rows_of = [slice(cc * c, (cc + 1) * c) for cc, _ in items]
    cols_of = [slice(hh * GDN_DK, (hh + 1) * GDN_DK) for _, hh in items]

    kh = each(lambda r, l: k_ref[r, l], rows_of, cols_of)
    khb = each(lambda x: x.astype(BF16), kh)
    beta = [gb_all[r, GDN_HEADS + hh:GDN_HEADS + hh + 1] for r, (_, hh) in zip(rows_of, items)]
    gcol = [gcum_all[r, hh:hh + 1] for r, (_, hh) in zip(rows_of, items)]
    grow = [gcum_t_all[hh:hh + 1, r] for r, (_, hh) in zip(rows_of, items)]
    decay = each(lambda gc_, gr_: jnp.where(causal, jnp.exp(jnp.where(causal, gc_ - gr_, 0.0)), 0.0), gcol, grow)
    kbeta = each(lambda x, b_: x * b_, kh, beta)
    a = each(lambda kb_, k_, d_: jnp.where(strict, _mm_nt(kb_.astype(BF16), k_) * d_, 0.0), kbeta, khb, decay)
    n1 = each(lambda a_: jnp.where(base, -a_, 0.0), a)
    n1b = each(lambda x: x.astype(BF16), n1)
    n2 = each(lambda x: _mm(x, x), n1b)
    npow = each(lambda x: x.astype(BF16), n2)
    tinv = each(lambda n1_, n2_, n1b_, n2b_: eye + n1_ + n2_ + _mm(n1b_, n2b_), n1, n2, n1b, npow)
    for _ in range(GDN_BASE.bit_length() - 3):
        npow = each(lambda x: _mm(x, x).astype(BF16), npow)
        tinv = each(lambda t_, p_: t_ + _mm(t_.astype(BF16), p_), tinv, npow)
    for m in levels:
        tb16 = each(lambda t_: t_.astype(BF16), tinv)
        mt = each(lambda a_, t_: _mm(jnp.where(m, a_, 0.0).astype(BF16), t_).astype(BF16), a, tb16)
        tinv = each(lambda t_, tb_, mt_: t_ - _mm(tb_, mt_), tinv, tb16, mt)
    tb16 = each(lambda t_: t_.astype(BF16), tinv)
    egc = each(jnp.exp, gcol)
    vh = each(lambda r, l: v_ref[r, l], rows_of, cols_of)
    rhs = each(lambda v_, b_, kb_, e_: jnp.concatenate([v_ * b_, kb_ * e_], axis=1).astype(BF16),
               vh, beta, kbeta, egc)
    uw = each(_mm, tb16, rhs)
    qh = each(lambda r, l: q_ref[r, l] * (GDN_DK ** -0.5), rows_of, cols_of)
    attn = each(lambda q_, k_, d_: jnp.where(causal, _mm_nt(q_.astype(BF16), k_) * d_, 0.0).astype(BF16),
                qh, khb, decay)
    for idx, (cc, hh) in enumerate(items):
        r, l = rows_of[idx], cols_of[idx]
        u_ref[r, l] = uw[idx][:, :GDN_DV]
        w_ref[r, l] = uw[idx][:, GDN_DV:].astype(BF16)
        attn_ref[r, hh * c:(hh + 1) * c] = attn[idx]
        glast = gcol[idx][c - 1:c, :]
        kd_ref[r, l] = (kh[idx] * jnp.exp(glast - gcol[idx])).astype(BF16)
        qd_ref[r, l] = (qh[idx] * egc[idx]).astype(BF16)


def _gdn_local_call(q, k, v, gb, chunks):
    n, w = q.shape
    tb = chunks * GDN_CHUNK
    blk = lambda width: pl.BlockSpec((tb, width), lambda i: (i, 0))
    out = lambda width, dt: jax.ShapeDtypeStruct((n, width), dt)
    return pl.pallas_call(
        functools.partial(_gdn_local_kernel, chunks=chunks),
        grid=(n // tb,),
        in_specs=[blk(w), blk(w), blk(w), blk(LANES)],
        out_specs=[blk(w), blk(w), blk(w), blk(w), blk(GDN_HEADS * GDN_CHUNK),
                   pl.BlockSpec((chunks, SUBLANES, LANES), lambda i: (i, 0, 0))],
        out_shape=[out(w, BF16), out(w, BF16), out(w, BF16), out(w, F32), out(GDN_HEADS * GDN_CHUNK, BF16),
                   jax.ShapeDtypeStruct((n // GDN_CHUNK, SUBLANES, LANES), F32)],
        compiler_params=_params("parallel"),
    )(q, k, v, gb)


def _gdn_scan_kernel(w_ref, qd_ref, kd_ref, u_ref, attn_ref, egl_ref, o_ref, state, *, chunks):
    c = GDN_CHUNK
    bsz = w_ref.shape[0]

    @pl.when(pl.program_id(0) == 0)
    def _():
        state[...] = jnp.zeros_like(state)

    items = [(b, hh) for b in range(bsz) for hh in range(GDN_HEADS)]
    each = lambda fn, *lists: [fn(*args) for args in zip(*lists)]
    cols_of = [slice(hh * GDN_DK, (hh + 1) * GDN_DK) for _, hh in items]
    sh = [state[idx] for idx in range(len(items))]
    for cc in range(chunks):
        rows = slice(cc * c, (cc + 1) * c)
        sb = each(lambda s_: s_.astype(BF16), sh)
        wq = [jnp.concatenate([w_ref[b, rows, l], qd_ref[b, rows, l]], axis=0) for (b, _), l in zip(items, cols_of)]
        r = each(_mm, wq, sb)
        vnb = [(u_ref[b, rows, l] - r_[:c]).astype(BF16) for (b, _), l, r_ in zip(items, cols_of, r)]
        for (b, hh), l, r_, v_ in zip(items, cols_of, r, vnb):
            o_ref[b, rows, l] = r_[c:] + _mm(attn_ref[b, rows, hh * c:(hh + 1) * c], v_)
        sh = [s_ * egl_ref[b, cc, hh:hh + 1, :] + _mm_tn(kd_ref[b, rows, l], v_)
              for (b, hh), l, s_, v_ in zip(items, cols_of, sh, vnb)]
    for idx, s_ in enumerate(sh):
        state[idx] = s_


def _gdn_scan_call(w, qd, kd, u, attn, egl, chunks):
    b, lp, width = u.shape
    tb = chunks * GDN_CHUNK
    blk = lambda wd: pl.BlockSpec((b, tb, wd), lambda j: (0, j, 0))
    return pl.pallas_call(
        functools.partial(_gdn_scan_kernel, chunks=chunks),
        grid=(lp // tb,),
        in_specs=[blk(width), blk(width), blk(width), blk(width), blk(GDN_HEADS * GDN_CHUNK),
                  pl.BlockSpec((b, chunks, SUBLANES, LANES), lambda j: (0, j, 0, 0))],
        out_specs=blk(width),
        out_shape=jax.ShapeDtypeStruct((b, lp, width), F32),
        scratch_shapes=[pltpu.VMEM((b * GDN_HEADS, GDN_DK, GDN_DV), F32)],
        compiler_params=_params("arbitrary"),
    )(w, qd, kd, u, attn, egl)


def _outproj_kernel(o_ref, sz_ref, yconv_ref, h_ref, wtop_ref, wbot_ref, nw_ref, g_ref, b_ref, out_ref, *, alpha):
    o = o_ref[...]
    parts = []
    for hh in range(GDN_HEADS):
        oh = o[:, hh * GDN_DV:(hh + 1) * GDN_DV]
        parts.append(oh * lax.rsqrt(jnp.mean(oh * oh, axis=-1, keepdims=True) + RMS_EPS) * nw_ref[...])
    ygdn = (jnp.concatenate(parts, axis=-1) * sz_ref[...].astype(F32)).astype(BF16)
    mix = _mm(yconv_ref[...], wtop_ref[...]) + _mm(ygdn, wbot_ref[...])
    out_ref[...] = _layer_norm(alpha * h_ref[...] + mix, g_ref[...], b_ref[...])


def _outproj_call(o, sz, yconv, h, wtop, wbot, nw, g, b, tb, alpha):
    n, d = h.shape
    full = lambda a: pl.BlockSpec(a.shape, lambda i: (0,) * a.ndim)
    blk = lambda w: pl.BlockSpec((tb, w), lambda i: (i, 0))
    return pl.pallas_call(
        functools.partial(_outproj_kernel, alpha=alpha),
        grid=(n // tb,),
        in_specs=[blk(GDN_WIDTH), blk(GDN_WIDTH), blk(CONV_WIDTH), blk(d), full(wtop), full(wbot), full(nw),
                  full(g), full(b)],
        out_specs=blk(d),
        out_shape=jax.ShapeDtypeStruct((n, d), F32),
        compiler_params=_params("parallel"),
    )(o, sz, yconv, h, wtop, wbot, nw, g, b)


def _top_values(s, n):
    tops = []
    cur = s
    for r in range(n):
        m = jnp.max(cur, axis=0, keepdims=True)
        tops.append(m)
        if r + 1 < n:
            cur = jnp.where(cur >= m, NEG_INF, cur)
    return jnp.concatenate(tops, axis=0)


def _peer_kernel(xa_ref, xc_ref, wq_ref, k1_ref, k2_ref, u_ref, vt_ref, g_ref, b_ref, out_ref,
                 xb_ref, acc_ref, ht_ref, coef_ref, th_ref, aa_ref, s2_ref, bb_ref, *, alpha, ec, ne, total):
    s = pl.program_id(0)
    tb = xa_ref.shape[0]
    nk = PEER_TOPK + 1
    chunk_a = jnp.clip(s, 0, total - 1) % ne
    chunk_b = (s - 1) % ne
    chunk_c = (s - 2) % ne

    @pl.when(s == 0)
    def _():
        ht_ref[...] = jnp.zeros_like(ht_ref)
        coef_ref[...] = jnp.zeros_like(coef_ref)
        th_ref[...] = jnp.zeros_like(th_ref)
        aa_ref[...] = jnp.zeros_like(aa_ref)
        s2_ref[...] = jnp.zeros_like(s2_ref)
        bb_ref[...] = jnp.zeros_like(bb_ref)
        acc_ref[...] = jnp.zeros_like(acc_ref)

    @pl.when(jnp.logical_and(s < total, chunk_a == 0))
    def _():
        xb_ref[...] = xa_ref[...].astype(BF16)

    @pl.when(jnp.logical_and(jnp.logical_and(s >= 1, s <= total), chunk_b == 0))
    def _():
        xb = xb_ref[...]
        for hd in range(PEER_HEADS):
            lo = hd * 2 * PEER_HALF
            q = _mm(xb, wq_ref[:, lo:lo + 2 * PEER_HALF]).astype(BF16)
            s1_all = _mm_nt(k1_ref[...], q[:, :PEER_HALF])
            s2_all = _mm_nt(k2_ref[...], q[:, PEER_HALF:])
            for lt in range(tb // LANES):
                ls = slice(lt * LANES, (lt + 1) * LANES)
                s1 = s1_all[:, ls]
                s2 = s2_all[:, ls]
                t1 = _top_values(s1, nk)
                t2 = _top_values(s2, nk)
                cand = jnp.concatenate([t1[r:r + 1] + t2[:nk // (r + 1)] for r in range(nk)], axis=0)
                best = _top_values(cand, nk)
                tau = 0.5 * (best[PEER_TOPK - 1:PEER_TOPK] + best[PEER_TOPK:PEER_TOPK + 1])
                z = jnp.sum(jnp.exp(best[:PEER_TOPK] - best[0:1]), axis=0, keepdims=True)
                th = tau - s1
                aa = jnp.exp(s1 - t1[0:1]) / z
                for grp in range(PEER_NKEYS // SUBLANES):
                    th_ref[hd, grp, :, ls] = th[grp * SUBLANES:(grp + 1) * SUBLANES]
                    aa_ref[hd, grp, :, ls] = aa[grp * SUBLANES:(grp + 1) * SUBLANES]
                s2_ref[hd, :, ls] = s2
                bb_ref[hd, :, ls] = jnp.exp(s2 - t2[0:1])

    @pl.when(jnp.logical_and(s >= 2, chunk_c == 0))
    def _():
        acc_ref[...] = jnp.zeros_like(acc_ref)

    rows_per = PEER_NKEYS
    assert ec == SUBLANES * rows_per

    def stage_a(ht_w, ts, ms):
        ht_w[ms, ts] = _mm_nt(u_ref[ms, :], xb_ref[ts, :])

    def stage_b(ht_r, coef_w, ig, jh, lt):
        ls = slice(lt * LANES, (lt + 1) * LANES)
        js = slice(jh * (rows_per // PEER_JSPLIT), (jh + 1) * (rows_per // PEER_JSPLIT))
        iis = range(ig * PEER_IGROUP, (ig + 1) * PEER_IGROUP)
        gsum = [jnp.zeros((rows_per // PEER_JSPLIT, LANES), F32) for _ in iis]
        for hd in range(PEER_HEADS):
            s2 = s2_ref[hd, js, ls]
            bb = bb_ref[hd, js, ls]
            for k, ii in enumerate(iis):
                th = th_ref[hd, chunk_b, ii:ii + 1, ls]
                aa = aa_ref[hd, chunk_b, ii:ii + 1, ls]
                gsum[k] = gsum[k] + jnp.where(s2 >= th, bb, 0.0) * aa
        for k, ii in enumerate(iis):
            rs = slice(ii * rows_per + js.start, ii * rows_per + js.stop)
            ht = ht_r[rs, ls]
            act = 0.5 * ht * (1.0 + lax.erf(ht * np.float32(np.sqrt(0.5))))
            coef_w[rs, ls] = (gsum[k] * act).astype(BF16)

    def stage_c(coef_r, ts, ms):
        acc_ref[ms, ts] += _mm(vt_ref[0, ms, :], coef_r[:, ts])

    assert acc_ref.shape[0] == ec
    mxu_pieces = [(slice(t0, t0 + PEER_SLICE), slice(m0, m0 + PEER_MROWS))
                  for t0 in range(0, tb, PEER_SLICE) for m0 in range(0, ec, PEER_MROWS)]
    valu_pieces = [(ig, jh, lt) for lt in range(tb // LANES) for ig in range(SUBLANES // PEER_IGROUP)
                   for jh in range(PEER_JSPLIT)]
    per = -(-len(valu_pieces) // len(mxu_pieces))

    def stages(ht_w, ht_r, coef_w, coef_r):
        for idx, (ts, ms) in enumerate(mxu_pieces):
            stage_a(ht_w, ts, ms)
            stage_c(coef_r, ts, ms)
            for piece in valu_pieces[idx * per:(idx + 1) * per]:
                stage_b(ht_r, coef_w, *piece)

    slot_a = s % 2
    slot_b = (s + 1) % 2
    stages(ht_ref.at[slot_a], ht_ref.at[slot_b], coef_ref.at[slot_b], coef_ref.at[slot_a])

    @pl.when(jnp.logical_and(s >= 2, chunk_c == ne - 1))
    def _():
        y = acc_ref[...].T
        out_ref[...] = _layer_norm(alpha * xc_ref[...] + y, g_ref[...], b_ref[...])


PEER_SLICE = 256
PEER_MROWS = 256
PEER_IGROUP = 4
PEER_JSPLIT = 2


def _peer_call(x, wq, k1, k2, u, vt, g, b, tb, ec, alpha):
    n, d = x.shape
    ne = u.shape[0] // ec
    total = (n // tb) * ne
    once = pl.Buffered(1)
    full = lambda a: pl.BlockSpec(a.shape, lambda s: (0,) * a.ndim, pipeline_mode=once)
    blk_a = lambda s: jnp.clip(s, 0, total - 1) // ne
    blk_c = lambda s: jnp.clip(s - 2, 0, total - 1) // ne
    head_scratch = pltpu.VMEM((PEER_HEADS, PEER_NKEYS, tb), F32)
    grouped_scratch = pltpu.VMEM((PEER_HEADS, PEER_NKEYS // SUBLANES, SUBLANES, tb), F32)
    return pl.pallas_call(
        functools.partial(_peer_kernel, alpha=alpha, ec=ec, ne=ne, total=total),
        grid=(total + 2,),
        in_specs=[pl.BlockSpec((tb, d), lambda s: (blk_a(s), 0), pipeline_mode=once),
                  pl.BlockSpec((tb, d), lambda s: (blk_c(s), 0), pipeline_mode=once),
                  full(wq), full(k1), full(k2),
                  pl.BlockSpec((ec, d), lambda s: (jnp.clip(s, 0, total - 1) % ne, 0)),
                  pl.BlockSpec((1, d, ec), lambda s: (jnp.clip(s - 2, 0, total - 1) % ne, 0, 0)),
                  full(g), full(b)],
        out_specs=pl.BlockSpec((tb, d), lambda s: (blk_c(s), 0)),
        out_shape=jax.ShapeDtypeStruct((n, d), F32),
        scratch_shapes=[pltpu.VMEM((tb, d), BF16), pltpu.VMEM((d, tb), F32),
                        pltpu.VMEM((2, ec, tb), F32), pltpu.VMEM((2, ec, tb), BF16),
                        grouped_scratch, grouped_scratch, head_scratch, head_scratch],
        compiler_params=_params("arbitrary"),
    )(x, x, wq, k1, k2, u, vt, g, b)


def kernel(x, meta_tokens, ln_in_g, ln_in_b, w_in, conv_w, gdn_conv_w, a_log, dt_bias, gdn_norm_w, w_out, ln1_g,
           ln1_b, peer_w_q, peer_k1, peer_k2, peer_u, peer_v, ln2_g, ln2_b):
    bsz, seq, d = x.shape
    depth = w_in.shape[0]
    alpha = (2.0 * depth) ** 0.25
    lp = SEQ_PAD + N_META + seq
    n = bsz * lp
    assert lp % GDN_CHUNK == 0

    tb_row = _pick_block(n, 768, LANES)
    tb_in = _pick_block(lp, 768, 2 * SUBLANES)
    chunks = 3 if (lp // GDN_CHUNK) % 3 == 0 else 1
    tb_peer = _pick_block(n, 768, PEER_SLICE)
    ec = SUBLANES * PEER_NKEYS
    n_exp = peer_u.shape[1]

    row = lambda a: a.reshape(1, -1).astype(F32)
    pad_lanes = lambda a: jnp.pad(a.reshape(1, -1).astype(F32), ((0, 0), (0, LANES - a.size)))

    meta = jnp.broadcast_to(meta_tokens[None].astype(x.dtype), (bsz, N_META, d))
    h = jnp.concatenate([jnp.zeros((bsz, SEQ_PAD, d), x.dtype), meta, x], axis=1).reshape(n, d)
    h = _ln_call(h, row(ln_in_g), row(ln_in_b), tb_row)

    c3 = 3 * CONV_WIDTH
    for l in range(depth):
        wl = w_in[l]
        wc = wl[:, :c3].astype(BF16)
        wqkv = wl[:, c3:c3 + QKV_COLS].astype(BF16)
        wz = wl[:, c3 + QKV_COLS:c3 + QKV_COLS + GDN_WIDTH].astype(BF16)
        wab = jnp.pad(wl[:, c3 + QKV_COLS + GDN_WIDTH:], ((0, 0), (0, LANES - 2 * GDN_HEADS))).astype(BF16)
        yconv, q, k, v, sz, gb = _inproj_call(
            h.reshape(bsz, lp, d), wc, wqkv, wz, wab, conv_w[l].astype(F32), gdn_conv_w[l].astype(F32),
            pad_lanes(a_log[l]), pad_lanes(dt_bias[l]), tb_in)
        flat = lambda a: a.reshape(n, a.shape[-1])
        wg, qd, kd, ug, attn, egl = _gdn_local_call(flat(q), flat(k), flat(v), flat(gb), chunks)
        seq = lambda a: a.reshape(bsz, lp, a.shape[-1])
        o = _gdn_scan_call(seq(wg), seq(qd), seq(kd), seq(ug), seq(attn),
                           egl.reshape(bsz, lp // GDN_CHUNK, SUBLANES, LANES), chunks)
        wo = w_out[l].astype(BF16)
        h = _outproj_call(o.reshape(n, GDN_WIDTH), sz.reshape(n, GDN_WIDTH), yconv.reshape(n, CONV_WIDTH), h,
                          wo[:CONV_WIDTH], wo[CONV_WIDTH:], row(gdn_norm_w[l]), row(ln1_g[l]), row(ln1_b[l]),
                          tb_row, alpha)
        ub = peer_u[l].astype(BF16)
        vt = peer_v[l].astype(BF16).reshape(n_exp // ec, ec, d).transpose(0, 2, 1)
        h = _peer_call(h, peer_w_q[l].astype(BF16), peer_k1[l].astype(BF16), peer_k2[l].astype(BF16), ub, vt,
                       row(ln2_g[l]), row(ln2_b[l]), tb_peer, ec, alpha)
    return h.reshape(bsz, lp, d)[:, SEQ_PAD + N_META:]
```

```python
import functools

import jax
import jax.numpy as jnp
import numpy as np
from jax import lax
from jax.experimental import pallas as pl
from jax.experimental.pallas import tpu as pltpu

F32 = jnp.float32
BF16 = jnp.bfloat16

N_META = 16
CONV_WIDTH = 512
CONV_K = 3
GDN_HEADS = 4
GDN_DK = 128
GDN_DV = 128
GDN_CONV_K = 4
GDN_CHUNK = 64
QKV_COLS = GDN_HEADS * (2 * GDN_DK + GDN_DV)
GDN_WIDTH = GDN_HEADS * GDN_DV
PEER_HEADS = 8
PEER_NKEYS = 128
PEER_HALF = 128
PEER_TOPK = 16
LN_EPS = 1e-5
RMS_EPS = 1e-6
SEQ_PAD = (-N_META) % GDN_CHUNK
LANES = 128
SUBLANES = 8
PACK = 16
VMEM_LIMIT = 56 * 1024 * 1024
NEG_INF = float("-inf")


def _mm(a, b, precision=None):
    return jnp.dot(a, b, preferred_element_type=F32, precision=precision)


def _mm_nt(a, b, precision=None):
    return lax.dot_general(a, b, (((1,), (1,)), ((), ())), preferred_element_type=F32, precision=precision)


def _mm_tn(a, b, precision=None):
    return lax.dot_general(a, b, (((0,), (0,)), ((), ())), preferred_element_type=F32, precision=precision)


def _layer_norm(x, g, b):
    mu = jnp.mean(x, axis=-1, keepdims=True)
    xc = x - mu
    var = jnp.mean(xc * xc, axis=-1, keepdims=True)
    return xc * lax.rsqrt(var + LN_EPS) * g + b


def _silu(x):
    return x * jax.nn.sigmoid(x)


def _pick_block(n, target, mult):
    best = None
    for d in range(mult, min(n, target) + 1, mult):
        if n % d == 0:
            best = d
    assert best is not None, (n, target, mult)
    return best


def _params(*sem):
    return pltpu.CompilerParams(dimension_semantics=sem, vmem_limit_bytes=VMEM_LIMIT)


def _ln_kernel(x_ref, g_ref, b_ref, o_ref):
    o_ref[...] = _layer_norm(x_ref[...], g_ref[...], b_ref[...])


def _ln_call(x2d, g, b, tb):
    n, d = x2d.shape
    return pl.pallas_call(
        _ln_kernel,
        grid=(n // tb,),
        in_specs=[pl.BlockSpec((tb, d), lambda i: (i, 0)),
                  pl.BlockSpec((1, d), lambda i: (0, 0)),
                  pl.BlockSpec((1, d), lambda i: (0, 0))],
        out_specs=pl.BlockSpec((tb, d), lambda i: (i, 0)),
        out_shape=jax.ShapeDtypeStruct((n, d), F32),
        compiler_params=_params("parallel"),
    )(x2d, g, b)


def _causal_conv(u, prev, w):
    k = w.shape[0]

    def taps(a):
        acc = a * w[k - 1:k]
        for s in range(1, k):
            acc = acc + pltpu.roll(a, s, 0) * w[k - 1 - s:k - s]
        return acc

    body = taps(u)
    head = taps(jnp.concatenate([prev, u[:SUBLANES]], axis=0))[SUBLANES:]
    return jnp.concatenate([head, body[SUBLANES:]], axis=0)


def _inproj_kernel(h_ref, wc_ref, wqkv_ref, wz_ref, wab_ref, cw_ref, gcw_ref, alog_ref, dtb_ref,
                   yconv_ref, q_ref, k_ref, v_ref, sz_ref, gb_ref, carry_c, carry_qkv):
    j = pl.program_id(1)
    tb = h_ref.shape[1]

    @pl.when(j == 0)
    def _():
        carry_c[...] = jnp.zeros_like(carry_c)
        carry_qkv[...] = jnp.zeros_like(carry_qkv)

    xb = h_ref[0].astype(BF16)
    row = j * tb + lax.broadcasted_iota(jnp.int32, (tb, 1), 0)
    valid = row >= SEQ_PAD

    p = _mm(xb, wc_ref[...])
    u = jnp.where(valid, p[:, CONV_WIDTH:2 * CONV_WIDTH] * p[:, 2 * CONV_WIDTH:], 0.0)
    yconv_ref[0] = (p[:, :CONV_WIDTH] * _causal_conv(u, carry_c[...], cw_ref[...])).astype(BF16)
    carry_c[...] = u[tb - SUBLANES:]

    p = jnp.where(valid, _mm(xb, wqkv_ref[...]), 0.0)
    s = _silu(_causal_conv(p, carry_qkv[...], gcw_ref[...]))
    carry_qkv[...] = p[tb - SUBLANES:]
    for hh in range(GDN_HEADS):
        lo = hh * GDN_DK
        qh = s[:, lo:lo + GDN_DK]
        kh = s[:, GDN_WIDTH + lo:GDN_WIDTH + lo + GDN_DK]
        q_ref[0, :, lo:lo + GDN_DK] = qh * lax.rsqrt(jnp.sum(qh * qh, axis=-1, keepdims=True) + RMS_EPS)
        k_ref[0, :, lo:lo + GDN_DK] = kh * lax.rsqrt(jnp.sum(kh * kh, axis=-1, keepdims=True) + RMS_EPS)
    v_ref[0] = s[:, 2 * GDN_WIDTH:]

    sz_ref[0] = _silu(_mm(xb, wz_ref[...])).astype(BF16)

    ab = _mm(xb, wab_ref[...])
    t = ab + dtb_ref[...]
    softplus = jnp.maximum(t, 0.0) + jnp.log1p(jnp.exp(-jnp.abs(t)))
    g = -jnp.exp(alog_ref[...]) * softplus
    lane = lax.broadcasted_iota(jnp.int32, (1, LANES), 1)
    gb = jnp.where(lane < GDN_HEADS, g, jax.nn.sigmoid(ab))
    gb_ref[0] = jnp.where(valid, gb, 0.0)


def _inproj_call(h, wc, wqkv, wz, wab, cw, gcw, alog, dtb, tb):
    b, lp, d = h.shape
    full = lambda a: pl.BlockSpec(a.shape, lambda bi, j: (0,) * a.ndim)
    blk = lambda w: pl.BlockSpec((1, tb, w), lambda bi, j: (bi, j, 0))
    out = lambda w, dt: jax.ShapeDtypeStruct((b, lp, w), dt)
    return pl.pallas_call(
        _inproj_kernel,
        grid=(b, lp // tb),
        in_specs=[blk(d), full(wc), full(wqkv), full(wz), full(wab), full(cw), full(gcw), full(alog), full(dtb)],
        out_specs=[blk(CONV_WIDTH), blk(GDN_WIDTH), blk(GDN_WIDTH), blk(GDN_WIDTH), blk(GDN_WIDTH), blk(LANES)],
        out_shape=[out(CONV_WIDTH, BF16), out(GDN_WIDTH, F32), out(GDN_WIDTH, F32), out(GDN_WIDTH, F32),
                   out(GDN_WIDTH, BF16), out(LANES, F32)],
        scratch_shapes=[pltpu.VMEM((SUBLANES, CONV_WIDTH), F32), pltpu.VMEM((SUBLANES, QKV_COLS), F32)],
        compiler_params=_params("parallel", "arbitrary"),
    )(h, wc, wqkv, wz, wab, cw, gcw, alog, dtb)


GDN_BASE = 16


def _gdn_local_kernel(q_ref, k_ref, v_ref, gb_ref, w_ref, qd_ref, kd_ref, u_ref, attn_ref, egl_ref, *, chunks):
    c = GDN_CHUNK
    hi = lax.Precision.HIGHEST
    rows_all = chunks * c

    ri = lax.broadcasted_iota(jnp.int32, (c, c), 0)
    ci = lax.broadcasted_iota(jnp.int32, (c, c), 1)
    causal = ri >= ci
    strict = ri > ci
    eye = (ri == ci).astype(F32)
    base = jnp.logical_and(strict, (ri // GDN_BASE) == (ci // GDN_BASE))
    levels = [jnp.logical_and(jnp.logical_and((ri >> (l + 1)) == (ci >> (l + 1)), ((ri >> l) & 1) == 1),
                              ((ci >> l) & 1) == 0) for l in range(GDN_BASE.bit_length() - 1, c.bit_length() - 1)]
    rr = lax.broadcasted_iota(jnp.int32, (rows_all, rows_all), 0)
    rc = lax.broadcasted_iota(jnp.int32, (rows_all, rows_all), 1)
    ltri = jnp.logical_and(rr >= rc, (rr // c) == (rc // c)).astype(F32)
    sel = (lax.broadcasted_iota(jnp.int32, (SUBLANES, LANES), 0)
           == lax.broadcasted_iota(jnp.int32, (SUBLANES, LANES), 1)).astype(F32)

    gb_all = gb_ref[...]
    gcum_all = _mm(ltri, gb_all, hi)
    gcum_t_all = _mm_nt(sel, gcum_all, hi)

    for cc in range(chunks):
        last = cc * c + c - 1
        egl_ref[cc] = jnp.exp(jnp.broadcast_to(gcum_t_all[:, last:last + 1], (SUBLANES, LANES)))

    items = [(cc, hh) for cc in range(chunks) for hh in range(GDN_HEADS)]
    each = lambda fn, *lists: [fn(*args) for args in zip(*lists)]
    rows_of = [slice(cc * c, (cc + 1) * c) for cc, _ in items]
    cols_of = [slice(hh * GDN_DK, (hh + 1) * GDN_DK) for _, hh in items]

    kh = each(lambda r, l: k_ref[r, l], rows_of, cols_of)
    khb = each(lambda x: x.astype(BF16), kh)
    beta = [gb_all[r, GDN_HEADS + hh:GDN_HEADS + hh + 1] for r, (_, hh) in zip(rows_of, items)]
    gcol = [gcum_all[r, hh:hh + 1] for r, (_, hh) in zip(rows_of, items)]
    grow = [gcum_t_all[hh:hh + 1, r] for r, (_, hh) in zip(rows_of, items)]
    decay = each(lambda gc_, gr_: jnp.where(causal, jnp.exp(jnp.where(causal, gc_ - gr_, 0.0)), 0.0), gcol, grow)
    kbeta = each(lambda x, b_: x * b_, kh, beta)
    a = each(lambda kb_, k_, d_: jnp.where(strict, _mm_nt(kb_.astype(BF16), k_) * d_, 0.0), kbeta, khb, decay)
    n1 = each(lambda a_: jnp.where(base, -a_, 0.0), a)
    n1b = each(lambda x: x.astype(BF16), n1)
    n2 = each(lambda x: _mm(x, x), n1b)
    npow = each(lambda x: x.astype(BF16), n2)
    tinv = each(lambda n1_, n2_, n1b_, n2b_: eye + n1_ + n2_ + _mm(n1b_, n2b_), n1, n2, n1b, npow)
    for _ in range(GDN_BASE.bit_length() - 3):
        npow = each(lambda x: _mm(x, x).astype(BF16), npow)
        tinv = each(lambda t_, p_: t_ + _mm(t_.astype(BF16), p_), tinv, npow)
    for m in levels:
        tb16 = each(lambda t_: t_.astype(BF16), tinv)
        mt = each(lambda a_, t_: _mm(jnp.where(m, a_, 0.0).astype(BF16), t_).astype(BF16), a, tb16)
        tinv = each(lambda t_, tb_, mt_: t_ - _mm(tb_, mt_), tinv, tb16, mt)
    tb16 = each(lambda t_: t_.astype(BF16), tinv)
    egc = each(jnp.exp, gcol)
    vh = each(lambda r, l: v_ref[r, l], rows_of, cols_of)
    rhs = each(lambda v_, b_, kb_, e_: jnp.concatenate([v_ * b_, kb_ * e_], axis=1).astype(BF16),
               vh, beta, kbeta, egc)
    uw = each(_mm, tb16, rhs)
    qh = each(lambda r, l: q_ref[r, l] * (GDN_DK ** -0.5), rows_of, cols_of)
    attn = each(lambda q_, k_, d_: jnp.where(causal, _mm_nt(q_.astype(BF16), k_) * d_, 0.0).astype(BF16),
                qh, khb, decay)
    for idx, (cc, hh) in enumerate(items):
        r, l = rows_of[idx], cols_of[idx]
        u_ref[r, l] = uw[idx][:, :GDN_DV]
        w_ref[r, l] = uw[idx][:, GDN_DV:].astype(BF16)
        attn_ref[r, hh * c:(hh + 1) * c] = attn[idx]
        glast = gcol[idx][c - 1:c, :]
        kd_ref[r, l] = (kh[idx] * jnp.exp(glast - gcol[idx])).astype(BF16)
        qd_ref[r, l] = (qh[idx] * egc[idx]).astype(BF16)


def _gdn_local_call(q, k, v, gb, chunks):
    n, w = q.shape
    tb = chunks * GDN_CHUNK
    blk = lambda width: pl.BlockSpec((tb, width), lambda i: (i, 0))
    out = lambda width, dt: jax.ShapeDtypeStruct((n, width), dt)
    return pl.pallas_call(
        functools.partial(_gdn_local_kernel, chunks=chunks),
        grid=(n // tb,),
        in_specs=[blk(w), blk(w), blk(w), blk(LANES)],
        out_specs=[blk(w), blk(w), blk(w), blk(w), blk(GDN_HEADS * GDN_CHUNK),
                   pl.BlockSpec((chunks, SUBLANES, LANES), lambda i: (i, 0, 0))],
        out_shape=[out(w, BF16), out(w, BF16), out(w, BF16), out(w, F32), out(GDN_HEADS * GDN_CHUNK, BF16),
                   jax.ShapeDtypeStruct((n // GDN_CHUNK, SUBLANES, LANES), F32)],
        compiler_params=_params("parallel"),
    )(q, k, v, gb)


def _gdn_scan_kernel(w_ref, qd_ref, kd_ref, u_ref, attn_ref, egl_ref, o_ref, state, *, chunks):
    c = GDN_CHUNK
    bsz = w_ref.shape[0]

    @pl.when(pl.program_id(0) == 0)
    def _():
        state[...] = jnp.zeros_like(state)

    items = [(b, hh) for b in range(bsz) for hh in range(GDN_HEADS)]
    each = lambda fn, *lists: [fn(*args) for args in zip(*lists)]
    cols_of = [slice(hh * GDN_DK, (hh + 1) * GDN_DK) for _, hh in items]
    sh = [state[idx] for idx in range(len(items))]
    for cc in range(chunks):
        rows = slice(cc * c, (cc + 1) * c)
        sb = each(lambda s_: s_.astype(BF16), sh)
        wq = [jnp.concatenate([w_ref[b, rows, l], qd_ref[b, rows, l]], axis=0) for (b, _), l in zip(items, cols_of)]
        r = each(_mm, wq, sb)
        vnb = [(u_ref[b, rows, l] - r_[:c]).astype(BF16) for (b, _), l, r_ in zip(items, cols_of, r)]
        for (b, hh), l, r_, v_ in zip(items, cols_of, r, vnb):
            o_ref[b, rows, l] = r_[c:] + _mm(attn_ref[b, rows, hh * c:(hh + 1) * c], v_)
        sh = [s_ * egl_ref[b, cc, hh:hh + 1, :] + _mm_tn(kd_ref[b, rows, l], v_)
              for (b, hh), l, s_, v_ in zip(items, cols_of, sh, vnb)]
    for idx, s_ in enumerate(sh):
        state[idx] = s_


def _gdn_scan_call(w, qd, kd, u, attn, egl, chunks):
    b, lp, width = u.shape
    tb = chunks * GDN_CHUNK
    blk = lambda wd: pl.BlockSpec((b, tb, wd), lambda j: (0, j, 0))
    return pl.pallas_call(
        functools.partial(_gdn_scan_kernel, chunks=chunks),
        grid=(lp // tb,),
        in_specs=[blk(width), blk(width), blk(width), blk(width), blk(GDN_HEADS * GDN_CHUNK),
                  pl.BlockSpec((b, chunks, SUBLANES, LANES), lambda j: (0, j, 0, 0))],
        out_specs=blk(width),
        out_shape=jax.ShapeDtypeStruct((b, lp, width), F32),
        scratch_shapes=[pltpu.VMEM((b * GDN_HEADS, GDN_DK, GDN_DV), F32)],
        compiler_params=_params("arbitrary"),
    )(w, qd, kd, u, attn, egl)


def _outproj_kernel(o_ref, sz_ref, yconv_ref, h_ref, wtop_ref, wbot_ref, nw_ref, g_ref, b_ref, out_ref, *, alpha):
    o = o_ref[...]
    parts = []
    for hh in range(GDN_HEADS):
        oh = o[:, hh * GDN_DV:(hh + 1) * GDN_DV]
        parts.append(oh * lax.rsqrt(jnp.mean(oh * oh, axis=-1, keepdims=True) + RMS_EPS) * nw_ref[...])
    ygdn = (jnp.concatenate(parts, axis=-1) * sz_ref[...].astype(F32)).astype(BF16)
    mix = _mm(yconv_ref[...], wtop_ref[...]) + _mm(ygdn, wbot_ref[...])
    out_ref[...] = _layer_norm(alpha * h_ref[...] + mix, g_ref[...], b_ref[...])


def _outproj_call(o, sz, yconv, h, wtop, wbot, nw, g, b, tb, alpha):
    n, d = h.shape
    full = lambda a: pl.BlockSpec(a.shape, lambda i: (0,) * a.ndim)
    blk = lambda w: pl.BlockSpec((tb, w), lambda i: (i, 0))
    return pl.pallas_call(
        functools.partial(_outproj_kernel, alpha=alpha),
        grid=(n // tb,),
        in_specs=[blk(GDN_WIDTH), blk(GDN_WIDTH), blk(CONV_WIDTH), blk(d), full(wtop), full(wbot), full(nw),
                  full(g), full(b)],
        out_specs=blk(d),
        out_shape=jax.ShapeDtypeStruct((n, d), F32),
        compiler_params=_params("parallel"),
    )(o, sz, yconv, h, wtop, wbot, nw, g, b)


def _top_values(s, n):
    tops = []
    cur = s
    for r in range(n):
        m = jnp.max(cur, axis=0, keepdims=True)
        tops.append(m)
        if r + 1 < n:
            cur = jnp.where(cur >= m, NEG_INF, cur)
    return jnp.concatenate(tops, axis=0)


def _peer_select_kernel(x_ref, wq_ref, k1_ref, k2_ref, xb_ref, th_ref, aa_ref, s2_ref, bb_ref):
    tb = x_ref.shape[0]
    nk = PEER_TOPK + 1
    xb = x_ref[...].astype(BF16)
    xb_ref[...] = xb
    if True:
        for hd in range(PEER_HEADS):
            lo = hd * 2 * PEER_HALF
            q = _mm(xb, wq_ref[:, lo:lo + 2 * PEER_HALF]).astype(BF16)
            s1_all = _mm_nt(k1_ref[...], q[:, :PEER_HALF])
            s2_all = _mm_nt(k2_ref[...], q[:, PEER_HALF:])
            for lt in range(tb // LANES):
                ls = slice(lt * LANES, (lt + 1) * LANES)
                s1 = s1_all[:, ls]
                s2 = s2_all[:, ls]
                t1 = _top_values(s1, nk)
                t2 = _top_values(s2, nk)
                cand = jnp.concatenate([t1[r:r + 1] + t2[:nk // (r + 1)] for r in range(nk)], axis=0)
                best = _top_values(cand, nk)
                tau = 0.5 * (best[PEER_TOPK - 1:PEER_TOPK] + best[PEER_TOPK:PEER_TOPK + 1])
                z = jnp.sum(jnp.exp(best[:PEER_TOPK] - best[0:1]), axis=0, keepdims=True)
                th = tau - s1
                aa = jnp.exp(s1 - t1[0:1]) / z
                for grp in range(PEER_NKEYS // SUBLANES):
                    th_ref[hd, grp, :, ls] = th[grp * SUBLANES:(grp + 1) * SUBLANES]
                    aa_ref[hd, grp, :, ls] = aa[grp * SUBLANES:(grp + 1) * SUBLANES]
                s2_ref[hd, :, ls] = s2
                bb_ref[hd, :, ls] = jnp.exp(s2 - t2[0:1])


def _peer_main_kernel(xb_ref, xc_ref, th_ref, aa_ref, s2_ref, bb_ref, u_ref, vt_ref, g_ref, b_ref, out_ref,
                      acc_ref, ht_ref, coef_ref, *, alpha, ec, ne, total):
    s = pl.program_id(0)
    tb, d = xb_ref.shape
    chunk_b = (s - 1) % ne
    chunk_c = (s - 2) % ne

    @pl.when(s == 0)
    def _():
        ht_ref[...] = jnp.zeros_like(ht_ref)
        coef_ref[...] = jnp.zeros_like(coef_ref)
        acc_ref[...] = jnp.zeros_like(acc_ref)

    @pl.when(jnp.logical_and(s >= 2, chunk_c == 0))
    def _():
        acc_ref[...] = jnp.zeros_like(acc_ref)

    rows_per = PEER_NKEYS
    assert ec == SUBLANES * rows_per

    def stage_b(ht_r, coef_w, ig, jh, lt):
        ls = slice(lt * LANES, (lt + 1) * LANES)
        js = slice(jh * (rows_per // PEER_JSPLIT), (jh + 1) * (rows_per // PEER_JSPLIT))
        iis = range(ig * PEER_IGROUP, (ig + 1) * PEER_IGROUP)
        gsum = [jnp.zeros((rows_per // PEER_JSPLIT, LANES), F32) for _ in iis]
        for hd in range(PEER_HEADS):
            s2 = s2_ref[hd, js, ls]
            bb = bb_ref[hd, js, ls]
            for k, ii in enumerate(iis):
                th = th_ref[hd, chunk_b, ii:ii + 1, ls]
                aa = aa_ref[hd, chunk_b, ii:ii + 1, ls]
                gsum[k] = gsum[k] + jnp.where(s2 >= th, bb, 0.0) * aa
        for k, ii in enumerate(iis):
            rs = slice(ii * rows_per + js.start, ii * rows_per + js.stop)
            ht = ht_r[rs, ls]
            act = 0.5 * ht * (1.0 + lax.erf(ht * np.float32(np.sqrt(0.5))))
            coef_w[rs, ls] = (gsum[k] * act).astype(BF16)

    assert acc_ref.shape[0] == ec and d == ec
    tiles = [(slice(t0, t0 + MXU_TILE), slice(k0, k0 + MXU_TILE))
             for t0 in range(0, tb, MXU_TILE) for k0 in range(0, d, MXU_TILE)]
    n_depth = d // MXU_TILE
    n_rows = ec // PEER_ROWCH
    valu_pieces = [(ig, jh, lt) for lt in range(tb // LANES) for ig in range(SUBLANES // PEER_IGROUP)
                   for jh in range(PEER_JSPLIT)]
    n_slots = len(tiles) * n_rows
    acc_entries = PEER_ROWCH // 4

    def stages(ht_w, ht_r, coef_w, coef_r):
        def push(i):
            ts, ks = tiles[i]
            pltpu.matmul_push_rhs(xb_ref[ts, ks], i % 2, MXU_A, transpose=True)
            pltpu.matmul_push_rhs(coef_r[ks, ts], i % 2, MXU_C)

        def pop(ts, r):
            rows = slice(r * PEER_ROWCH, (r + 1) * PEER_ROWCH)
            shape = (PEER_ROWCH, MXU_TILE)
            ht_w[rows, ts] = pltpu.matmul_pop(r * acc_entries, shape, F32, MXU_A)
            acc_ref[rows, ts] += pltpu.matmul_pop(r * acc_entries, shape, F32, MXU_C)

        push(0)
        pending = []
        done = 0
        for i, (ts, ks) in enumerate(tiles):
            for r in range(n_rows):
                slot = i * n_rows + r
                while pending and pending[0][0] <= slot:
                    pop(*pending.pop(0)[1:])
                rows = slice(r * PEER_ROWCH, (r + 1) * PEER_ROWCH)
                staged = i % 2 if r == 0 else None
                pltpu.matmul_acc_lhs(r * acc_entries, u_ref[rows, ks], MXU_A, load_staged_rhs=staged)
                pltpu.matmul_acc_lhs(r * acc_entries, vt_ref[0, rows, ks], MXU_C, load_staged_rhs=staged)
                if r == 0 and i + 1 < len(tiles):
                    push(i + 1)
                if (i + 1) % n_depth == 0:
                    pending.append((slot + PEER_POP_LAG, ts, r))
                upto = (slot + 1) * len(valu_pieces) // n_slots
                for piece in valu_pieces[done:upto]:
                    stage_b(ht_r, coef_w, *piece)
                done = upto
        for _, ts, r in pending:
            pop(ts, r)

    slot_a = s % 2
    slot_b = (s + 1) % 2
    stages(ht_ref.at[slot_a], ht_ref.at[slot_b], coef_ref.at[slot_b], coef_ref.at[slot_a])

    @pl.when(jnp.logical_and(s >= 2, chunk_c == ne - 1))
    def _():
        y = acc_ref[...].T
        out_ref[...] = _layer_norm(alpha * xc_ref[...] + y, g_ref[...], b_ref[...])


MXU_TILE = 256
MXU_A, MXU_C = 1, 0
PEER_ROWCH = 128
PEER_POP_LAG = 4
PEER_IGROUP = 4
PEER_JSPLIT = 2


def _peer_select_call(x, wq, k1, k2, tb):
    n, d = x.shape
    full = lambda a: pl.BlockSpec(a.shape, lambda i: (0,) * a.ndim)
    grouped = (PEER_HEADS, PEER_NKEYS // SUBLANES, SUBLANES)
    return pl.pallas_call(
        _peer_select_kernel,
        grid=(n // tb,),
        in_specs=[pl.BlockSpec((tb, d), lambda i: (i, 0)), full(wq), full(k1), full(k2)],
        out_specs=[pl.BlockSpec((tb, d), lambda i: (i, 0)),
                   pl.BlockSpec(grouped + (tb,), lambda i: (0, 0, 0, i)),
                   pl.BlockSpec(grouped + (tb,), lambda i: (0, 0, 0, i)),
                   pl.BlockSpec((PEER_HEADS, PEER_NKEYS, tb), lambda i: (0, 0, i)),
                   pl.BlockSpec((PEER_HEADS, PEER_NKEYS, tb), lambda i: (0, 0, i))],
        out_shape=[jax.ShapeDtypeStruct((n, d), BF16),
                   jax.ShapeDtypeStruct(grouped + (n,), F32), jax.ShapeDtypeStruct(grouped + (n,), F32),
                   jax.ShapeDtypeStruct((PEER_HEADS, PEER_NKEYS, n), F32),
                   jax.ShapeDtypeStruct((PEER_HEADS, PEER_NKEYS, n), F32)],
        compiler_params=_params("parallel"),
    )(x, wq, k1, k2)


def _peer_main_call(x, xb, th, aa, s2, bb, u, vt, g, b, tb, ec, alpha):
    n, d = x.shape
    ne = u.shape[0] // ec
    total = (n // tb) * ne
    once = pl.Buffered(1)
    full = lambda a: pl.BlockSpec(a.shape, lambda s: (0,) * a.ndim, pipeline_mode=once)
    blk = lambda lag: (lambda s: jnp.clip(s - lag, 0, total - 1) // ne)
    grouped = (PEER_HEADS, PEER_NKEYS // SUBLANES, SUBLANES)
    rows4 = lambda: pl.BlockSpec(grouped + (tb,), lambda s: (0, 0, 0, blk(1)(s)), pipeline_mode=once)
    rows3 = lambda: pl.BlockSpec((PEER_HEADS, PEER_NKEYS, tb), lambda s: (0, 0, blk(1)(s)), pipeline_mode=once)
    return pl.pallas_call(
        functools.partial(_peer_main_kernel, alpha=alpha, ec=ec, ne=ne, total=total),
        grid=(total + 2,),
        in_specs=[pl.BlockSpec((tb, d), lambda s: (blk(0)(s), 0), pipeline_mode=once),
                  pl.BlockSpec((tb, d), lambda s: (blk(2)(s), 0), pipeline_mode=once),
                  rows4(), rows4(), rows3(), rows3(),
                  pl.BlockSpec((ec, d), lambda s: (jnp.clip(s, 0, total - 1) % ne, 0)),
                  pl.BlockSpec((1, d, ec), lambda s: (jnp.clip(s - 2, 0, total - 1) % ne, 0, 0)),
                  full(g), full(b)],
        out_specs=pl.BlockSpec((tb, d), lambda s: (blk(2)(s), 0)),
        out_shape=jax.ShapeDtypeStruct((n, d), F32),
        scratch_shapes=[pltpu.VMEM((d, tb), F32), pltpu.VMEM((2, ec, tb), F32), pltpu.VMEM((2, ec, tb), BF16)],
        compiler_params=_params("arbitrary"),
    )(xb, x, th, aa, s2, bb, u, vt, g, b)


def kernel(x, meta_tokens, ln_in_g, ln_in_b, w_in, conv_w, gdn_conv_w, a_log, dt_bias, gdn_norm_w, w_out, ln1_g,
           ln1_b, peer_w_q, peer_k1, peer_k2, peer_u, peer_v, ln2_g, ln2_b):
    bsz, seq, d = x.shape
    depth = w_in.shape[0]
    alpha = (2.0 * depth) ** 0.25
    lp = SEQ_PAD + N_META + seq
    n = bsz * lp
    assert lp % GDN_CHUNK == 0

    tb_row = _pick_block(n, 768, LANES)
    tb_in = _pick_block(lp, 768, 2 * SUBLANES)
    chunks = 3 if (lp // GDN_CHUNK) % 3 == 0 else 1
    tb_peer = _pick_block(n, 768, MXU_TILE)
    ec = SUBLANES * PEER_NKEYS
    n_exp = peer_u.shape[1]

    row = lambda a: a.reshape(1, -1).astype(F32)
    pad_lanes = lambda a: jnp.pad(a.reshape(1, -1).astype(F32), ((0, 0), (0, LANES - a.size)))

    meta = jnp.broadcast_to(meta_tokens[None].astype(x.dtype), (bsz, N_META, d))
    h = jnp.concatenate([jnp.zeros((bsz, SEQ_PAD, d), x.dtype), meta, x], axis=1).reshape(n, d)
    h = _ln_call(h, row(ln_in_g), row(ln_in_b), tb_row)

    c3 = 3 * CONV_WIDTH
    for l in range(depth):
        wl = w_in[l]
        wc = wl[:, :c3].astype(BF16)
        wqkv = wl[:, c3:c3 + QKV_COLS].astype(BF16)
        wz = wl[:, c3 + QKV_COLS:c3 + QKV_COLS + GDN_WIDTH].astype(BF16)
        wab = jnp.pad(wl[:, c3 + QKV_COLS + GDN_WIDTH:], ((0, 0), (0, LANES - 2 * GDN_HEADS))).astype(BF16)
        yconv, q, k, v, sz, gb = _inproj_call(
            h.reshape(bsz, lp, d), wc, wqkv, wz, wab, conv_w[l].astype(F32), gdn_conv_w[l].astype(F32),
            pad_lanes(a_log[l]), pad_lanes(dt_bias[l]), tb_in)
        flat = lambda a: a.reshape(n, a.shape[-1])
        wg, qd, kd, ug, attn, egl = _gdn_local_call(flat(q), flat(k), flat(v), flat(gb), chunks)
        seq = lambda a: a.reshape(bsz, lp, a.shape[-1])
        o = _gdn_scan_call(seq(wg), seq(qd), seq(kd), seq(ug), seq(attn),
                           egl.reshape(bsz, lp // GDN_CHUNK, SUBLANES, LANES), chunks)
        wo = w_out[l].astype(BF16)
        h = _outproj_call(o.reshape(n, GDN_WIDTH), sz.reshape(n, GDN_WIDTH), yconv.reshape(n, CONV_WIDTH), h,
                          wo[:CONV_WIDTH], wo[CONV_WIDTH:], row(gdn_norm_w[l]), row(ln1_g[l]), row(ln1_b[l]),
                          tb_row, alpha)
        ub = peer_u[l].astype(BF16)
        vt = peer_v[l].astype(BF16).reshape(n_exp // ec, ec, d).transpose(0, 2, 1)
        xb, th, aa, s2, bb = _peer_select_call(h, peer_w_q[l].astype(BF16), peer_k1[l].astype(BF16),
                                               peer_k2[l].astype(BF16), tb_peer)
        h = _peer_main_call(h, xb, th, aa, s2, bb, ub, vt, row(ln2_g[l]), row(ln2_b[l]), tb_peer, ec, alpha)
    return h.reshape(bsz, lp, d)[:, SEQ_PAD + N_META:]
```

```python
import functools

import jax
import jax.numpy as jnp
import numpy as np
from jax import lax
from jax.experimental import pallas as pl
from jax.experimental.pallas import tpu as pltpu

F32 = jnp.float32
BF16 = jnp.bfloat16

N_META = 16
CONV_WIDTH = 512
CONV_K = 3
GDN_HEADS = 4
GDN_DK = 128
GDN_DV = 128
GDN_CONV_K = 4
GDN_CHUNK = 64
QKV_COLS = GDN_HEADS * (2 * GDN_DK + GDN_DV)
GDN_WIDTH = GDN_HEADS * GDN_DV
PEER_HEADS = 8
PEER_NKEYS = 128
PEER_HALF = 128
PEER_TOPK = 16
LN_EPS = 1e-5
RMS_EPS = 1e-6
SEQ_PAD = (-N_META) % GDN_CHUNK
LANES = 128
SUBLANES = 8
VMEM_LIMIT = 56 * 1024 * 1024
NEG_INF = float("-inf")


def _mm(a, b, precision=None):
    return jnp.dot(a, b, preferred_element_type=F32, precision=precision)


def _mm_nt(a, b, precision=None):
    return lax.dot_general(a, b, (((1,), (1,)), ((), ())), preferred_element_type=F32, precision=precision)


def _mm_tn(a, b, precision=None):
    return lax.dot_general(a, b, (((0,), (0,)), ((), ())), preferred_element_type=F32, precision=precision)


def _layer_norm(x, g, b):
    mu = jnp.mean(x, axis=-1, keepdims=True)
    xc = x - mu
    var = jnp.mean(xc * xc, axis=-1, keepdims=True)
    return xc * lax.rsqrt(var + LN_EPS) * g + b


def _silu(x):
    return x * jax.nn.sigmoid(x)


def _pick_block(n, target, mult):
    best = None
    for d in range(mult, min(n, target) + 1, mult):
        if n % d == 0:
            best = d
    assert best is not None, (n, target, mult)
    return best


def _params(*sem):
    return pltpu.CompilerParams(dimension_semantics=sem, vmem_limit_bytes=VMEM_LIMIT)


def _ln_kernel(x_ref, g_ref, b_ref, o_ref):
    o_ref[...] = _layer_norm(x_ref[...], g_ref[...], b_ref[...])


def _ln_call(x2d, g, b, tb):
    n, d = x2d.shape
    return pl.pallas_call(
        _ln_kernel,
        grid=(n // tb,),
        in_specs=[pl.BlockSpec((tb, d), lambda i: (i, 0)),
                  pl.BlockSpec((1, d), lambda i: (0, 0)),
                  pl.BlockSpec((1, d), lambda i: (0, 0))],
        out_specs=pl.BlockSpec((tb, d), lambda i: (i, 0)),
        out_shape=jax.ShapeDtypeStruct((n, d), F32),
        compiler_params=_params("parallel"),
    )(x2d, g, b)


def _causal_conv(u, prev, w):
    k = w.shape[0]

    def taps(a):
        acc = a * w[k - 1:k]
        for s in range(1, k):
            acc = acc + pltpu.roll(a, s, 0) * w[k - 1 - s:k - s]
        return acc

    body = taps(u)
    head = taps(jnp.concatenate([prev, u[:SUBLANES]], axis=0))[SUBLANES:]
    return jnp.concatenate([head, body[SUBLANES:]], axis=0)


def _inproj_kernel(h_ref, wc_ref, wqkv_ref, wz_ref, wab_ref, cw_ref, gcw_ref, alog_ref, dtb_ref,
                   yconv_ref, q_ref, k_ref, v_ref, sz_ref, gb_ref, carry_c, carry_qkv):
    j = pl.program_id(1)
    tb = h_ref.shape[1]

    @pl.when(j == 0)
    def _():
        carry_c[...] = jnp.zeros_like(carry_c)
        carry_qkv[...] = jnp.zeros_like(carry_qkv)

    xb = h_ref[0].astype(BF16)
    row = j * tb + lax.broadcasted_iota(jnp.int32, (tb, 1), 0)
    valid = row >= SEQ_PAD

    p = _mm(xb, wc_ref[...])
    u = jnp.where(valid, p[:, CONV_WIDTH:2 * CONV_WIDTH] * p[:, 2 * CONV_WIDTH:], 0.0)
    yconv_ref[0] = (p[:, :CONV_WIDTH] * _causal_conv(u, carry_c[...], cw_ref[...])).astype(BF16)
    carry_c[...] = u[tb - SUBLANES:]

    p = jnp.where(valid, _mm(xb, wqkv_ref[...]), 0.0)
    s = _silu(_causal_conv(p, carry_qkv[...], gcw_ref[...]))
    carry_qkv[...] = p[tb - SUBLANES:]
    for hh in range(GDN_HEADS):
        lo = hh * GDN_DK
        qh = s[:, lo:lo + GDN_DK]
        kh = s[:, GDN_WIDTH + lo:GDN_WIDTH + lo + GDN_DK]
        q_ref[0, :, lo:lo + GDN_DK] = qh * lax.rsqrt(jnp.sum(qh * qh, axis=-1, keepdims=True) + RMS_EPS)
        k_ref[0, :, lo:lo + GDN_DK] = kh * lax.rsqrt(jnp.sum(kh * kh, axis=-1, keepdims=True) + RMS_EPS)
    v_ref[0] = s[:, 2 * GDN_WIDTH:]

    sz_ref[0] = _silu(_mm(xb, wz_ref[...])).astype(BF16)

    ab = _mm(xb, wab_ref[...])
    t = ab + dtb_ref[...]
    softplus = jnp.maximum(t, 0.0) + jnp.log1p(jnp.exp(-jnp.abs(t)))
    g = -jnp.exp(alog_ref[...]) * softplus
    lane = lax.broadcasted_iota(jnp.int32, (1, LANES), 1)
    gb = jnp.where(lane < GDN_HEADS, g, jax.nn.sigmoid(ab))
    gb_ref[0] = jnp.where(valid, gb, 0.0)


def _inproj_call(h, wc, wqkv, wz, wab, cw, gcw, alog, dtb, tb):
    b, lp, d = h.shape
    full = lambda a: pl.BlockSpec(a.shape, lambda bi, j: (0,) * a.ndim)
    blk = lambda w: pl.BlockSpec((1, tb, w), lambda bi, j: (bi, j, 0))
    out = lambda w, dt: jax.ShapeDtypeStruct((b, lp, w), dt)
    return pl.pallas_call(
        _inproj_kernel,
        grid=(b, lp // tb),
        in_specs=[blk(d), full(wc), full(wqkv), full(wz), full(wab), full(cw), full(gcw), full(alog), full(dtb)],
        out_specs=[blk(CONV_WIDTH), blk(GDN_WIDTH), blk(GDN_WIDTH), blk(GDN_WIDTH), blk(GDN_WIDTH), blk(LANES)],
        out_shape=[out(CONV_WIDTH, BF16), out(GDN_WIDTH, F32), out(GDN_WIDTH, F32), out(GDN_WIDTH, F32),
                   out(GDN_WIDTH, BF16), out(LANES, F32)],
        scratch_shapes=[pltpu.VMEM((SUBLANES, CONV_WIDTH), F32), pltpu.VMEM((SUBLANES, QKV_COLS), F32)],
        compiler_params=_params("parallel", "arbitrary"),
    )(h, wc, wqkv, wz, wab, cw, gcw, alog, dtb)


GDN_BASE = 16


def _gdn_local_kernel(q_ref, k_ref, v_ref, gb_ref, w_ref, qd_ref, kd_ref, u_ref, attn_ref, egl_ref, *, chunks):
    c = GDN_CHUNK
    hi = lax.Precision.HIGHEST
    rows_all = chunks * c

    ri = lax.broadcasted_iota(jnp.int32, (c, c), 0)
    ci = lax.broadcasted_iota(jnp.int32, (c, c), 1)
    causal = ri >= ci
    strict = ri > ci
    eye = (ri == ci).astype(F32)
    base = jnp.logical_and(strict, (ri // GDN_BASE) == (ci // GDN_BASE))
    levels = [jnp.logical_and(jnp.logical_and((ri >> (l + 1)) == (ci >> (l + 1)), ((ri >> l) & 1) == 1),
                              ((ci >> l) & 1) == 0) for l in range(GDN_BASE.bit_length() - 1, c.bit_length() - 1)]
    rr = lax.broadcasted_iota(jnp.int32, (rows_all, rows_all), 0)
    rc = lax.broadcasted_iota(jnp.int32, (rows_all, rows_all), 1)
    ltri = jnp.logical_and(rr >= rc, (rr // c) == (rc // c)).astype(F32)
    sel = (lax.broadcasted_iota(jnp.int32, (SUBLANES, LANES), 0)
           == lax.broadcasted_iota(jnp.int32, (SUBLANES, LANES), 1)).astype(F32)

    gb_all = gb_ref[...]
    gcum_all = _mm(ltri, gb_all, hi)
    gcum_t_all = _mm_nt(sel, gcum_all, hi)

    for cc in range(chunks):
        last = cc * c + c - 1
        egl_ref[cc] = jnp.exp(jnp.broadcast_to(gcum_t_all[:, last:last + 1], (SUBLANES, LANES)))

    items = [(cc, hh) for cc in range(chunks) for hh in range(GDN_HEADS)]
    each = lambda fn, *lists: [fn(*args) for args in zip(*lists)]
    rows_of = [slice(cc * c, (cc + 1) * c) for cc, _ in items]
    cols_of = [slice(hh * GDN_DK, (hh + 1) * GDN_DK) for _, hh in items]

    kh = each(lambda r, l: k_ref[r, l], rows_of, cols_of)
    khb = each(lambda x: x.astype(BF16), kh)
    beta = [gb_all[r, GDN_HEADS + hh:GDN_HEADS + hh + 1] for r, (_, hh) in zip(rows_of, items)]
    gcol = [gcum_all[r, hh:hh + 1] for r, (_, hh) in zip(rows_of, items)]
    grow = [gcum_t_all[hh:hh + 1, r] for r, (_, hh) in zip(rows_of, items)]
    decay = each(lambda gc_, gr_: jnp.where(causal, jnp.exp(jnp.where(causal, gc_ - gr_, 0.0)), 0.0), gcol, grow)
    kbeta = each(lambda x, b_: x * b_, kh, beta)
    a = each(lambda kb_, k_, d_: jnp.where(strict, _mm_nt(kb_.astype(BF16), k_) * d_, 0.0), kbeta, khb, decay)
    n1 = each(lambda a_: jnp.where(base, -a_, 0.0), a)
    n1b = each(lambda x: x.astype(BF16), n1)
    n2 = each(lambda x: _mm(x, x), n1b)
    npow = each(lambda x: x.astype(BF16), n2)
    tinv = each(lambda n1_, n2_, n1b_, n2b_: eye + n1_ + n2_ + _mm(n1b_, n2b_), n1, n2, n1b, npow)
    for _ in range(GDN_BASE.bit_length() - 3):
        npow = each(lambda x: _mm(x, x).astype(BF16), npow)
        tinv = each(lambda t_, p_: t_ + _mm(t_.astype(BF16), p_), tinv, npow)
    for m in levels:
        tb16 = each(lambda t_: t_.astype(BF16), tinv)
        mt = each(lambda a_, t_: _mm(jnp.where(m, a_, 0.0).astype(BF16), t_).astype(BF16), a, tb16)
        tinv = each(lambda t_, tb_, mt_: t_ - _mm(tb_, mt_), tinv, tb16, mt)
    tb16 = each(lambda t_: t_.astype(BF16), tinv)
    egc = each(jnp.exp, gcol)
    vh = each(lambda r, l: v_ref[r, l], rows_of, cols_of)
    rhs = each(lambda v_, b_, kb_, e_: jnp.concatenate([v_ * b_, kb_ * e_], axis=1).astype(BF16),
               vh, beta, kbeta, egc)
    uw = each(_mm, tb16, rhs)
    qh = each(lambda r, l: q_ref[r, l] * (GDN_DK ** -0.5), rows_of, cols_of)
    attn = each(lambda q_, k_, d_: jnp.where(causal, _mm_nt(q_.astype(BF16), k_) * d_, 0.0).astype(BF16),
                qh, khb, decay)
    for idx, (cc, hh) in enumerate(items):
        r, l = rows_of[idx], cols_of[idx]
        u_ref[r, l] = uw[idx][:, :GDN_DV]
        w_ref[r, l] = uw[idx][:, GDN_DV:].astype(BF16)
        attn_ref[r, hh * c:(hh + 1) * c] = attn[idx]
        glast = gcol[idx][c - 1:c, :]
        kd_ref[r, l] = (kh[idx] * jnp.exp(glast - gcol[idx])).astype(BF16)
        qd_ref[r, l] = (qh[idx] * egc[idx]).astype(BF16)


def _gdn_local_call(q, k, v, gb, chunks):
    n, w = q.shape
    tb = chunks * GDN_CHUNK
    blk = lambda width: pl.BlockSpec((tb, width), lambda i: (i, 0))
    out = lambda width, dt: jax.ShapeDtypeStruct((n, width), dt)
    return pl.pallas_call(
        functools.partial(_gdn_local_kernel, chunks=chunks),
        grid=(n // tb,),
        in_specs=[blk(w), blk(w), blk(w), blk(LANES)],
        out_specs=[blk(w), blk(w), blk(w), blk(w), blk(GDN_HEADS * GDN_CHUNK),
                   pl.BlockSpec((chunks, SUBLANES, LANES), lambda i: (i, 0, 0))],
        out_shape=[out(w, BF16), out(w, BF16), out(w, BF16), out(w, F32), out(GDN_HEADS * GDN_CHUNK, BF16),
                   jax.ShapeDtypeStruct((n // GDN_CHUNK, SUBLANES, LANES), F32)],
        compiler_params=_params("parallel"),
    )(q, k, v, gb)


def _gdn_scan_kernel(w_ref, qd_ref, kd_ref, u_ref, attn_ref, egl_ref, o_ref, state, *, chunks):
    c = GDN_CHUNK
    bsz = w_ref.shape[0]

    @pl.when(pl.program_id(0) == 0)
    def _():
        state[...] = jnp.zeros_like(state)

    items = [(b, hh) for b in range(bsz) for hh in range(GDN_HEADS)]
    each = lambda fn, *lists: [fn(*args) for args in zip(*lists)]
    cols_of = [slice(hh * GDN_DK, (hh + 1) * GDN_DK) for _, hh in items]
    sh = [state[idx] for idx in range(len(items))]
    for cc in range(chunks):
        rows = slice(cc * c, (cc + 1) * c)
        sb = each(lambda s_: s_.astype(BF16), sh)
        wq = [jnp.concatenate([w_ref[b, rows, l], qd_ref[b, rows, l]], axis=0) for (b, _), l in zip(items, cols_of)]
        r = each(_mm, wq, sb)
        vnb = [(u_ref[b, rows, l] - r_[:c]).astype(BF16) for (b, _), l, r_ in zip(items, cols_of, r)]
        for (b, hh), l, r_, v_ in zip(items, cols_of, r, vnb):
            o_ref[b, rows, l] = r_[c:] + _mm(attn_ref[b, rows, hh * c:(hh + 1) * c], v_)
        sh = [s_ * egl_ref[b, cc, hh:hh + 1, :] + _mm_tn(kd_ref[b, rows, l], v_)
              for (b, hh), l, s_, v_ in zip(items, cols_of, sh, vnb)]
    for idx, s_ in enumerate(sh):
        state[idx] = s_


def _gdn_scan_call(w, qd, kd, u, attn, egl, chunks):
    b, lp, width = u.shape
    tb = chunks * GDN_CHUNK
    blk = lambda wd: pl.BlockSpec((b, tb, wd), lambda j: (0, j, 0))
    return pl.pallas_call(
        functools.partial(_gdn_scan_kernel, chunks=chunks),
        grid=(lp // tb,),
        in_specs=[blk(width), blk(width), blk(width), blk(width), blk(GDN_HEADS * GDN_CHUNK),
                  pl.BlockSpec((b, chunks, SUBLANES, LANES), lambda j: (0, j, 0, 0))],
        out_specs=blk(width),
        out_shape=jax.ShapeDtypeStruct((b, lp, width), F32),
        scratch_shapes=[pltpu.VMEM((b * GDN_HEADS, GDN_DK, GDN_DV), F32)],
        compiler_params=_params("arbitrary"),
    )(w, qd, kd, u, attn, egl)


def _outproj_kernel(o_ref, sz_ref, yconv_ref, h_ref, wtop_ref, wbot_ref, nw_ref, g_ref, b_ref, out_ref, *, alpha):
    o = o_ref[...]
    parts = []
    for hh in range(GDN_HEADS):
        oh = o[:, hh * GDN_DV:(hh + 1) * GDN_DV]
        parts.append(oh * lax.rsqrt(jnp.mean(oh * oh, axis=-1, keepdims=True) + RMS_EPS) * nw_ref[...])
    ygdn = (jnp.concatenate(parts, axis=-1) * sz_ref[...].astype(F32)).astype(BF16)
    mix = _mm(yconv_ref[...], wtop_ref[...]) + _mm(ygdn, wbot_ref[...])
    out_ref[...] = _layer_norm(alpha * h_ref[...] + mix, g_ref[...], b_ref[...])


def _outproj_call(o, sz, yconv, h, wtop, wbot, nw, g, b, tb, alpha):
    n, d = h.shape
    full = lambda a: pl.BlockSpec(a.shape, lambda i: (0,) * a.ndim)
    blk = lambda w: pl.BlockSpec((tb, w), lambda i: (i, 0))
    return pl.pallas_call(
        functools.partial(_outproj_kernel, alpha=alpha),
        grid=(n // tb,),
        in_specs=[blk(GDN_WIDTH), blk(GDN_WIDTH), blk(CONV_WIDTH), blk(d), full(wtop), full(wbot), full(nw),
                  full(g), full(b)],
        out_specs=blk(d),
        out_shape=jax.ShapeDtypeStruct((n, d), F32),
        compiler_params=_params("parallel"),
    )(o, sz, yconv, h, wtop, wbot, nw, g, b)


def _top_values(s, n):
    tops = []
    cur = s
    for r in range(n):
        m = jnp.max(cur, axis=0, keepdims=True)
        tops.append(m)
        if r + 1 < n:
            cur = jnp.where(cur >= m, NEG_INF, cur)
    return jnp.concatenate(tops, axis=0)


def _sorting_network(n):
    pairs = []
    p = 1
    while p < n:
        k = p
        while k >= 1:
            for j in range(k % p, n - k, 2 * k):
                for i in range(min(k, n - j - k)):
                    if (i + j) // (2 * p) == (i + j + k) // (2 * p):
                        pairs.append((i + j, i + j + k))
            k //= 2
        p *= 2
    return pairs


def _top_values_sorted(s, n):
    groups = s.shape[0] // SUBLANES
    cols = [s[g * SUBLANES:(g + 1) * SUBLANES] for g in range(groups)]
    for a, b in _sorting_network(groups):
        cols[a], cols[b] = jnp.maximum(cols[a], cols[b]), jnp.minimum(cols[a], cols[b])
    tops = []
    for r in range(n):
        m = jnp.max(cols[0], axis=0, keepdims=True)
        tops.append(m)
        if r + 1 < n:
            hit = cols[0] >= m
            for k in range(min(groups, n - 1 - r)):
                below = cols[k + 1] if k + 1 < groups else NEG_INF
                cols[k] = jnp.where(hit, below, cols[k])
    return jnp.concatenate(tops, axis=0)


def _peer_kernel(xa_ref, xc_ref, wq_ref, k1_ref, k2_ref, u_ref, vt_ref, g_ref, b_ref, out_ref,
                 xb_ref, acc_ref, ht_ref, coef_ref, th_ref, aa_ref, s2_ref, bb_ref, *, alpha, ec, ne, total):
    s = pl.program_id(0)
    tb = xa_ref.shape[0]
    nk = PEER_TOPK + 1
    chunk_a = jnp.clip(s, 0, total - 1) % ne
    chunk_b = (s - 1) % ne
    chunk_c = (s - 2) % ne

    @pl.when(s == 0)
    def _():
        ht_ref[...] = jnp.zeros_like(ht_ref)
        coef_ref[...] = jnp.zeros_like(coef_ref)
        th_ref[...] = jnp.zeros_like(th_ref)
        aa_ref[...] = jnp.zeros_like(aa_ref)
        s2_ref[...] = jnp.zeros_like(s2_ref)
        bb_ref[...] = jnp.zeros_like(bb_ref)
        acc_ref[...] = jnp.zeros_like(acc_ref)

    @pl.when(jnp.logical_and(s < total, chunk_a == 0))
    def _():
        xb_ref[...] = xa_ref[...].astype(BF16)

    @pl.when(jnp.logical_and(jnp.logical_and(s >= 1, s <= total), chunk_b == 0))
    def _():
        xb = xb_ref[...]
        for hd in range(PEER_HEADS):
            lo = hd * 2 * PEER_HALF
            q = _mm(xb, wq_ref[:, lo:lo + 2 * PEER_HALF]).astype(BF16)
            s1_all = _mm_nt(k1_ref[...], q[:, :PEER_HALF])
            s2_all = _mm_nt(k2_ref[...], q[:, PEER_HALF:])
            for lt in range(tb // LANES):
                ls = slice(lt * LANES, (lt + 1) * LANES)
                s1 = s1_all[:, ls]
                s2 = s2_all[:, ls]
                t1 = _top_values_sorted(s1, nk)
                t2 = _top_values_sorted(s2, nk)
                cand = jnp.concatenate([t1[r:r + 1] + t2[:nk // (r + 1)] for r in range(nk)], axis=0)
                best = _top_values(cand, nk)
                tau = 0.5 * (best[PEER_TOPK - 1:PEER_TOPK] + best[PEER_TOPK:PEER_TOPK + 1])
                z = jnp.sum(jnp.exp(best[:PEER_TOPK] - best[0:1]), axis=0, keepdims=True)
                th = tau - s1
                aa = jnp.exp(s1 - t1[0:1]) / z
                for grp in range(PEER_NKEYS // SUBLANES):
                    th_ref[hd, grp, :, ls] = th[grp * SUBLANES:(grp + 1) * SUBLANES]
                    aa_ref[hd, grp, :, ls] = aa[grp * SUBLANES:(grp + 1) * SUBLANES]
                s2_ref[hd, :, ls] = s2
                bb_ref[hd, :, ls] = jnp.exp(s2 - t2[0:1])

    @pl.when(jnp.logical_and(s >= 2, chunk_c == 0))
    def _():
        acc_ref[...] = jnp.zeros_like(acc_ref)

    rows_per = PEER_NKEYS
    assert ec == SUBLANES * rows_per

    def stage_a(ht_w, ts, ms):
        ht_w[ms, ts] = _mm_nt(u_ref[ms, :], xb_ref[ts, :])

    def stage_b(ht_r, coef_w, ig, jh, lt):
        ls = slice(lt * LANES, (lt + 1) * LANES)
        js = slice(jh * (rows_per // PEER_JSPLIT), (jh + 1) * (rows_per // PEER_JSPLIT))
        iis = range(ig * PEER_IGROUP, (ig + 1) * PEER_IGROUP)
        gsum = [jnp.zeros((rows_per // PEER_JSPLIT, LANES), F32) for _ in iis]
        for hd in range(PEER_HEADS):
            s2 = s2_ref[hd, js, ls]
            bb = bb_ref[hd, js, ls]
            for k, ii in enumerate(iis):
                th = th_ref[hd, chunk_b, ii:ii + 1, ls]
                aa = aa_ref[hd, chunk_b, ii:ii + 1, ls]
                gsum[k] = gsum[k] + jnp.where(s2 >= th, bb, 0.0) * aa
        for k, ii in enumerate(iis):
            rs = slice(ii * rows_per + js.start, ii * rows_per + js.stop)
            ht = ht_r[rs, ls]
            act = 0.5 * ht * (1.0 + lax.erf(ht * np.float32(np.sqrt(0.5))))
            coef_w[rs, ls] = (gsum[k] * act).astype(BF16)

    def stage_c(coef_r, ts, ms):
        acc_ref[ms, ts] += _mm(vt_ref[0, ms, :], coef_r[:, ts])

    assert acc_ref.shape[0] == ec
    mxu_pieces = [(slice(t0, t0 + PEER_SLICE), slice(m0, m0 + PEER_MROWS))
                  for t0 in range(0, tb, PEER_SLICE) for m0 in range(0, ec, PEER_MROWS)]
    valu_pieces = [(ig, jh, lt) for lt in range(tb // LANES) for ig in range(SUBLANES // PEER_IGROUP)
                   for jh in range(PEER_JSPLIT)]
    per = -(-len(valu_pieces) // len(mxu_pieces))

    def stages(ht_w, ht_r, coef_w, coef_r):
        for idx, (ts, ms) in enumerate(mxu_pieces):
            stage_a(ht_w, ts, ms)
            stage_c(coef_r, ts, ms)
            for piece in valu_pieces[idx * per:(idx + 1) * per]:
                stage_b(ht_r, coef_w, *piece)

    slot_a = s % 2
    slot_b = (s + 1) % 2
    stages(ht_ref.at[slot_a], ht_ref.at[slot_b], coef_ref.at[slot_b], coef_ref.at[slot_a])

    @pl.when(jnp.logical_and(s >= 2, chunk_c == ne - 1))
    def _():
        y = acc_ref[...].T
        out_ref[...] = _layer_norm(alpha * xc_ref[...] + y, g_ref[...], b_ref[...])


PEER_SLICE = 256
PEER_MROWS = 256
PEER_IGROUP = 4
PEER_JSPLIT = 2


def _peer_call(x, wq, k1, k2, u, vt, g, b, tb, ec, alpha):
    n, d = x.shape
    ne = u.shape[0] // ec
    total = (n // tb) * ne
    once = pl.Buffered(1)
    full = lambda a: pl.BlockSpec(a.shape, lambda s: (0,) * a.ndim, pipeline_mode=once)
    blk_a = lambda s: jnp.clip(s, 0, total - 1) // ne
    blk_c = lambda s: jnp.clip(s - 2, 0, total - 1) // ne
    head_scratch = pltpu.VMEM((PEER_HEADS, PEER_NKEYS, tb), F32)
    grouped_scratch = pltpu.VMEM((PEER_HEADS, PEER_NKEYS // SUBLANES, SUBLANES, tb), F32)
    return pl.pallas_call(
        functools.partial(_peer_kernel, alpha=alpha, ec=ec, ne=ne, total=total),
        grid=(total + 2,),
        in_specs=[pl.BlockSpec((tb, d), lambda s: (blk_a(s), 0), pipeline_mode=once),
                  pl.BlockSpec((tb, d), lambda s: (blk_c(s), 0), pipeline_mode=once),
                  full(wq), full(k1), full(k2),
                  pl.BlockSpec((ec, d), lambda s: (jnp.clip(s, 0, total - 1) % ne, 0)),
                  pl.BlockSpec((1, d, ec), lambda s: (jnp.clip(s - 2, 0, total - 1) % ne, 0, 0)),
                  full(g), full(b)],
        out_specs=pl.BlockSpec((tb, d), lambda s: (blk_c(s), 0)),
        out_shape=jax.ShapeDtypeStruct((n, d), F32),
        scratch_shapes=[pltpu.VMEM((tb, d), BF16), pltpu.VMEM((d, tb), F32),
                        pltpu.VMEM((2, ec, tb), F32), pltpu.VMEM((2, ec, tb), BF16),
                        grouped_scratch, grouped_scratch, head_scratch, head_scratch],
        compiler_params=_params("arbitrary"),
    )(x, x, wq, k1, k2, u, vt, g, b)


def kernel(x, meta_tokens, ln_in_g, ln_in_b, w_in, conv_w, gdn_conv_w, a_log, dt_bias, gdn_norm_w, w_out, ln1_g,
           ln1_b, peer_w_q, peer_k1, peer_k2, peer_u, peer_v, ln2_g, ln2_b):
    bsz, seq, d = x.shape
    depth = w_in.shape[0]
    alpha = (2.0 * depth) ** 0.25
    lp = SEQ_PAD + N_META + seq
    n = bsz * lp
    assert lp % GDN_CHUNK == 0

    tb_row = _pick_block(n, 768, LANES)
    tb_in = _pick_block(lp, 768, 2 * SUBLANES)
    chunks = 3 if (lp // GDN_CHUNK) % 3 == 0 else 1
    tb_peer = _pick_block(n, 768, PEER_SLICE)
    ec = SUBLANES * PEER_NKEYS
    n_exp = peer_u.shape[1]

    row = lambda a: a.reshape(1, -1).astype(F32)
    pad_lanes = lambda a: jnp.pad(a.reshape(1, -1).astype(F32), ((0, 0), (0, LANES - a.size)))

    meta = jnp.broadcast_to(meta_tokens[None].astype(x.dtype), (bsz, N_META, d))
    h = jnp.concatenate([jnp.zeros((bsz, SEQ_PAD, d), x.dtype), meta, x], axis=1).reshape(n, d)
    h = _ln_call(h, row(ln_in_g), row(ln_in_b), tb_row)

    c3 = 3 * CONV_WIDTH
    for l in range(depth):
        wl = w_in[l]
        wc = wl[:, :c3].astype(BF16)
        wqkv = wl[:, c3:c3 + QKV_COLS].astype(BF16)
        wz = wl[:, c3 + QKV_COLS:c3 + QKV_COLS + GDN_WIDTH].astype(BF16)
        wab = jnp.pad(wl[:, c3 + QKV_COLS + GDN_WIDTH:], ((0, 0), (0, LANES - 2 * GDN_HEADS))).astype(BF16)
        yconv, q, k, v, sz, gb = _inproj_call(
            h.reshape(bsz, lp, d), wc, wqkv, wz, wab, conv_w[l].astype(F32), gdn_conv_w[l].astype(F32),
            pad_lanes(a_log[l]), pad_lanes(dt_bias[l]), tb_in)
        flat = lambda a: a.reshape(n, a.shape[-1])
        wg, qd, kd, ug, attn, egl = _gdn_local_call(flat(q), flat(k), flat(v), flat(gb), chunks)
        seq = lambda a: a.reshape(bsz, lp, a.shape[-1])
        o = _gdn_scan_call(seq(wg), seq(qd), seq(kd), seq(ug), seq(attn),
                           egl.reshape(bsz, lp // GDN_CHUNK, SUBLANES, LANES), chunks)
        wo = w_out[l].astype(BF16)
        h = _outproj_call(o.reshape(n, GDN_WIDTH), sz.reshape(n, GDN_WIDTH), yconv.reshape(n, CONV_WIDTH), h,
                          wo[:CONV_WIDTH], wo[CONV_WIDTH:], row(gdn_norm_w[l]), row(ln1_g[l]), row(ln1_b[l]),
                          tb_row, alpha)
        ub = peer_u[l].astype(BF16)
        vt = peer_v[l].astype(BF16).reshape(n_exp // ec, ec, d).transpose(0, 2, 1)
        h = _peer_call(h, peer_w_q[l].astype(BF16), peer_k1[l].astype(BF16), peer_k2[l].astype(BF16), ub, vt,
                       row(ln2_g[l]), row(ln2_b[l]), tb_peer, ec, alpha)
    return h.reshape(bsz, lp, d)[:, SEQ_PAD + N_META:]
```

```python
import functools

import jax
import jax.numpy as jnp
import numpy as np
from jax import lax
from jax.experimental import pallas as pl
from jax.experimental.pallas import tpu as pltpu

F32 = jnp.float32
BF16 = jnp.bfloat16

N_META = 16
CONV_WIDTH = 512
CONV_K = 3
GDN_HEADS = 4
GDN_DK = 128
GDN_DV = 128
GDN_CONV_K = 4
GDN_CHUNK = 64
QKV_COLS = GDN_HEADS * (2 * GDN_DK + GDN_DV)
GDN_WIDTH = GDN_HEADS * GDN_DV
PEER_HEADS = 8
PEER_NKEYS = 128
PEER_HALF = 128
PEER_TOPK = 16
LN_EPS = 1e-5
RMS_EPS = 1e-6
SEQ_PAD = (-N_META) % GDN_CHUNK
LANES = 128
SUBLANES = 8
VMEM_LIMIT = 56 * 1024 * 1024
NEG_INF = float("-inf")


def _mm(a, b, precision=None):
    return jnp.dot(a, b, preferred_element_type=F32, precision=precision)


def _mm_nt(a, b, precision=None):
    return lax.dot_general(a, b, (((1,), (1,)), ((), ())), preferred_element_type=F32, precision=precision)


def _mm_tn(a, b, precision=None):
    return lax.dot_general(a, b, (((0,), (0,)), ((), ())), preferred_element_type=F32, precision=precision)


def _layer_norm(x, g, b):
    mu = jnp.mean(x, axis=-1, keepdims=True)
    xc = x - mu
    var = jnp.mean(xc * xc, axis=-1, keepdims=True)
    return xc * lax.rsqrt(var + LN_EPS) * g + b


def _silu(x):
    return x * jax.nn.sigmoid(x)


def _pick_block(n, target, mult):
    best = None
    for d in range(mult, min(n, target) + 1, mult):
        if n % d == 0:
            best = d
    assert best is not None, (n, target, mult)
    return best


def _params(*sem):
    return pltpu.CompilerParams(dimension_semantics=sem, vmem_limit_bytes=VMEM_LIMIT)


def _ln_kernel(x_ref, g_ref, b_ref, o_ref):
    o_ref[...] = _layer_norm(x_ref[...], g_ref[...], b_ref[...])


def _ln_call(x2d, g, b, tb):
    n, d = x2d.shape
    return pl.pallas_call(
        _ln_kernel,
        grid=(n // tb,),
        in_specs=[pl.BlockSpec((tb, d), lambda i: (i, 0)),
                  pl.BlockSpec((1, d), lambda i: (0, 0)),
                  pl.BlockSpec((1, d), lambda i: (0, 0))],
        out_specs=pl.BlockSpec((tb, d), lambda i: (i, 0)),
        out_shape=jax.ShapeDtypeStruct((n, d), F32),
        compiler_params=_params("parallel"),
    )(x2d, g, b)


def _causal_conv(u, prev, w):
    k = w.shape[0]

    def taps(a):
        acc = a * w[k - 1:k]
        for s in range(1, k):
            acc = acc + pltpu.roll(a, s, 0) * w[k - 1 - s:k - s]
        return acc

    body = taps(u)
    head = taps(jnp.concatenate([prev, u[:SUBLANES]], axis=0))[SUBLANES:]
    return jnp.concatenate([head, body[SUBLANES:]], axis=0)


def _inproj_kernel(h_ref, wc_ref, wqkv_ref, wz_ref, wab_ref, cw_ref, gcw_ref, alog_ref, dtb_ref,
                   yconv_ref, q_ref, k_ref, v_ref, sz_ref, gb_ref, carry_c, carry_qkv):
    j = pl.program_id(1)
    tb = h_ref.shape[1]

    @pl.when(j == 0)
    def _():
        carry_c[...] = jnp.zeros_like(carry_c)
        carry_qkv[...] = jnp.zeros_like(carry_qkv)

    xb = h_ref[0].astype(BF16)
    row = j * tb + lax.broadcasted_iota(jnp.int32, (tb, 1), 0)
    valid = row >= SEQ_PAD

    p = _mm(xb, wc_ref[...])
    u = jnp.where(valid, p[:, CONV_WIDTH:2 * CONV_WIDTH] * p[:, 2 * CONV_WIDTH:], 0.0)
    yconv_ref[0] = (p[:, :CONV_WIDTH] * _causal_conv(u, carry_c[...], cw_ref[...])).astype(BF16)
    carry_c[...] = u[tb - SUBLANES:]

    p = jnp.where(valid, _mm(xb, wqkv_ref[...]), 0.0)
    s = _silu(_causal_conv(p, carry_qkv[...], gcw_ref[...]))
    carry_qkv[...] = p[tb - SUBLANES:]
    for hh in range(GDN_HEADS):
        lo = hh * GDN_DK
        qh = s[:, lo:lo + GDN_DK]
        kh = s[:, GDN_WIDTH + lo:GDN_WIDTH + lo + GDN_DK]
        q_ref[0, :, lo:lo + GDN_DK] = qh * lax.rsqrt(jnp.sum(qh * qh, axis=-1, keepdims=True) + RMS_EPS)
        k_ref[0, :, lo:lo + GDN_DK] = kh * lax.rsqrt(jnp.sum(kh * kh, axis=-1, keepdims=True) + RMS_EPS)
    v_ref[0] = s[:, 2 * GDN_WIDTH:]

    sz_ref[0] = _silu(_mm(xb, wz_ref[...])).astype(BF16)

    ab = _mm(xb, wab_ref[...])
    t = ab + dtb_ref[...]
    softplus = jnp.maximum(t, 0.0) + jnp.log1p(jnp.exp(-jnp.abs(t)))
    g = -jnp.exp(alog_ref[...]) * softplus
    lane = lax.broadcasted_iota(jnp.int32, (1, LANES), 1)
    gb = jnp.where(lane < GDN_HEADS, g, jax.nn.sigmoid(ab))
    gb_ref[0] = jnp.where(valid, gb, 0.0)


def _inproj_call(h, wc, wqkv, wz, wab, cw, gcw, alog, dtb, tb):
    b, lp, d = h.shape
    full = lambda a: pl.BlockSpec(a.shape, lambda bi, j: (0,) * a.ndim)
    blk = lambda w: pl.BlockSpec((1, tb, w), lambda bi, j: (bi, j, 0))
    out = lambda w, dt: jax.ShapeDtypeStruct((b, lp, w), dt)
    return pl.pallas_call(
        _inproj_kernel,
        grid=(b, lp // tb),
        in_specs=[blk(d), full(wc), full(wqkv), full(wz), full(wab), full(cw), full(gcw), full(alog), full(dtb)],
        out_specs=[blk(CONV_WIDTH), blk(GDN_WIDTH), blk(GDN_WIDTH), blk(GDN_WIDTH), blk(GDN_WIDTH), blk(LANES)],
        out_shape=[out(CONV_WIDTH, BF16), out(GDN_WIDTH, F32), out(GDN_WIDTH, F32), out(GDN_WIDTH, F32),
                   out(GDN_WIDTH, BF16), out(LANES, F32)],
        scratch_shapes=[pltpu.VMEM((SUBLANES, CONV_WIDTH), F32), pltpu.VMEM((SUBLANES, QKV_COLS), F32)],
        compiler_params=_params("parallel", "arbitrary"),
    )(h, wc, wqkv, wz, wab, cw, gcw, alog, dtb)


GDN_BASE = 16


def _gdn_local_kernel(q_ref, k_ref, v_ref, gb_ref, w_ref, qd_ref, kd_ref, u_ref, attn_ref, egl_ref, *, chunks):
    c = GDN_CHUNK
    hi = lax.Precision.HIGHEST
    rows_all = chunks * c

    ri = lax.broadcasted_iota(jnp.int32, (c, c), 0)
    ci = lax.broadcasted_iota(jnp.int32, (c, c), 1)
    causal = ri >= ci
    strict = ri > ci
    eye = (ri == ci).astype(F32)
    base = jnp.logical_and(strict, (ri // GDN_BASE) == (ci // GDN_BASE))
    levels = [jnp.logical_and(jnp.logical_and((ri >> (l + 1)) == (ci >> (l + 1)), ((ri >> l) & 1) == 1),
                              ((ci >> l) & 1) == 0) for l in range(GDN_BASE.bit_length() - 1, c.bit_length() - 1)]
    rr = lax.broadcasted_iota(jnp.int32, (rows_all, rows_all), 0)
    rc = lax.broadcasted_iota(jnp.int32, (rows_all, rows_all), 1)
    ltri = jnp.logical_and(rr >= rc, (rr // c) == (rc // c)).astype(F32)
    sel = (lax.broadcasted_iota(jnp.int32, (SUBLANES, LANES), 0)
           == lax.broadcasted_iota(jnp.int32, (SUBLANES, LANES), 1)).astype(F32)

    gb_all = gb_ref[...]
    gcum_all = _mm(ltri, gb_all, hi)
    gcum_t_all = _mm_nt(sel, gcum_all, hi)

    for cc in range(chunks):
        last = cc * c + c - 1
        egl_ref[cc] = jnp.exp(jnp.broadcast_to(gcum_t_all[:, last:last + 1], (SUBLANES, LANES)))

    items = [(cc, hh) for cc in range(chunks) for hh in range(GDN_HEADS)]
    each = lambda fn, *lists: [fn(*args) for args in zip(*lists)]
    rows_of = [slice(cc * c, (cc + 1) * c) for cc, _ in items]
    cols_of = [slice(hh * GDN_DK, (hh + 1) * GDN_DK) for _, hh in items]

    kh = each(lambda r, l: k_ref[r, l], rows_of, cols_of)
    khb = each(lambda x: x.astype(BF16), kh)
    beta = [gb_all[r, GDN_HEADS + hh:GDN_HEADS + hh + 1] for r, (_, hh) in zip(rows_of, items)]
    gcol = [gcum_all[r, hh:hh + 1] for r, (_, hh) in zip(rows_of, items)]
    grow = [gcum_t_all[hh:hh + 1, r] for r, (_, hh) in zip(rows_of, items)]
    decay = each(lambda gc_, gr_: jnp.where(causal, jnp.exp(jnp.where(causal, gc_ - gr_, 0.0)), 0.0), gcol, grow)
    kbeta = each(lambda x, b_: x * b_, kh, beta)
    a = each(lambda kb_, k_, d_: jnp.where(strict, _mm_nt(kb_.astype(BF16), k_) * d_, 0.0), kbeta, khb, decay)
    n1 = each(lambda a_: jnp.where(base, -a_, 0.0), a)
    n1b = each(lambda x: x.astype(BF16), n1)
    n2 = each(lambda x: _mm(x, x), n1b)
    npow = each(lambda x: x.astype(BF16), n2)
    tinv = each(lambda n1_, n2_, n1b_, n2b_: eye + n1_ + n2_ + _mm(n1b_, n2b_), n1, n2, n1b, npow)
    for _ in range(GDN_BASE.bit_length() - 3):
        npow = each(lambda x: _mm(x, x).astype(BF16), npow)
        tinv = each(lambda t_, p_: t_ + _mm(t_.astype(BF16), p_), tinv, npow)
    for m in levels:
        tb16 = each(lambda t_: t_.astype(BF16), tinv)
        mt = each(lambda a_, t_: _mm(jnp.where(m, a_, 0.0).astype(BF16), t_).astype(BF16), a, tb16)
        tinv = each(lambda t_, tb_, mt_: t_ - _mm(tb_, mt_), tinv, tb16, mt)
    tb16 = each(lambda t_: t_.astype(BF16), tinv)
    egc = each(jnp.exp, gcol)
    vh = each(lambda r, l: v_ref[r, l], rows_of, cols_of)
    rhs = each(lambda v_, b_, kb_, e_: jnp.concatenate([v_ * b_, kb_ * e_], axis=1).astype(BF16),
               vh, beta, kbeta, egc)
    uw = each(_mm, tb16, rhs)
    qh = each(lambda r, l: q_ref[r, l] * (GDN_DK ** -0.5), rows_of, cols_of)
    attn = each(lambda q_, k_, d_: jnp.where(causal, _mm_nt(q_.astype(BF16), k_) * d_, 0.0).astype(BF16),
                qh, khb, decay)
    for idx, (cc, hh) in enumerate(items):
        r, l = rows_of[idx], cols_of[idx]
        u_ref[r, l] = uw[idx][:, :GDN_DV]
        w_ref[r, l] = uw[idx][:, GDN_DV:].astype(BF16)
        attn_ref[r, hh * c:(hh + 1) * c] = attn[idx]
        glast = gcol[idx][c - 1:c, :]
        kd_ref[r, l] = (kh[idx] * jnp.exp(glast - gcol[idx])).astype(BF16)
        qd_ref[r, l] = (qh[idx] * egc[idx]).astype(BF16)


def _gdn_local_call(q, k, v, gb, chunks):
    n, w = q.shape
    tb = chunks * GDN_CHUNK
    blk = lambda width: pl.BlockSpec((tb, width), lambda i: (i, 0))
    out = lambda width, dt: jax.ShapeDtypeStruct((n, width), dt)
    return pl.pallas_call(
        functools.partial(_gdn_local_kernel, chunks=chunks),
        grid=(n // tb,),
        in_specs=[blk(w), blk(w), blk(w), blk(LANES)],
        out_specs=[blk(w), blk(w), blk(w), blk(w), blk(GDN_HEADS * GDN_CHUNK),
                   pl.BlockSpec((chunks, SUBLANES, LANES), lambda i: (i, 0, 0))],
        out_shape=[out(w, BF16), out(w, BF16), out(w, BF16), out(w, F32), out(GDN_HEADS * GDN_CHUNK, BF16),
                   jax.ShapeDtypeStruct((n // GDN_CHUNK, SUBLANES, LANES), F32)],
        compiler_params=_params("parallel"),
    )(q, k, v, gb)


def _gdn_scan_kernel(w_ref, qd_ref, kd_ref, u_ref, attn_ref, egl_ref, o_ref, state, *, chunks):
    c = GDN_CHUNK
    bsz = w_ref.shape[0]

    @pl.when(pl.program_id(0) == 0)
    def _():
        state[...] = jnp.zeros_like(state)

    items = [(b, hh) for b in range(bsz) for hh in range(GDN_HEADS)]
    each = lambda fn, *lists: [fn(*args) for args in zip(*lists)]
    cols_of = [slice(hh * GDN_DK, (hh + 1) * GDN_DK) for _, hh in items]
    sh = [state[idx] for idx in range(len(items))]
    for cc in range(chunks):
        rows = slice(cc * c, (cc + 1) * c)
        sb = each(lambda s_: s_.astype(BF16), sh)
        wq = [jnp.concatenate([w_ref[b, rows, l], qd_ref[b, rows, l]], axis=0) for (b, _), l in zip(items, cols_of)]
        r = each(_mm, wq, sb)
        vnb = [(u_ref[b, rows, l] - r_[:c]).astype(BF16) for (b, _), l, r_ in zip(items, cols_of, r)]
        for (b, hh), l, r_, v_ in zip(items, cols_of, r, vnb):
            o_ref[b, rows, l] = r_[c:] + _mm(attn_ref[b, rows, hh * c:(hh + 1) * c], v_)
        sh = [s_ * egl_ref[b, cc, hh:hh + 1, :] + _mm_tn(kd_ref[b, rows, l], v_)
              for (b, hh), l, s_, v_ in zip(items, cols_of, sh, vnb)]
    for idx, s_ in enumerate(sh):
        state[idx] = s_


def _gdn_scan_call(w, qd, kd, u, attn, egl, chunks):
    b, lp, width = u.shape
    tb = chunks * GDN_CHUNK
    blk = lambda wd: pl.BlockSpec((b, tb, wd), lambda j: (0, j, 0))
    return pl.pallas_call(
        functools.partial(_gdn_scan_kernel, chunks=chunks),
        grid=(lp // tb,),
        in_specs=[blk(width), blk(width), blk(width), blk(width), blk(GDN_HEADS * GDN_CHUNK),
                  pl.BlockSpec((b, chunks, SUBLANES, LANES), lambda j: (0, j, 0, 0))],
        out_specs=blk(width),
        out_shape=jax.ShapeDtypeStruct((b, lp, width), F32),
        scratch_shapes=[pltpu.VMEM((b * GDN_HEADS, GDN_DK, GDN_DV), F32)],
        compiler_params=_params("arbitrary"),
    )(w, qd, kd, u, attn, egl)


def _outproj_kernel(o_ref, sz_ref, yconv_ref, h_ref, wtop_ref, wbot_ref, nw_ref, g_ref, b_ref, out_ref, *, alpha):
    o = o_ref[...]
    parts = []
    for hh in range(GDN_HEADS):
        oh = o[:, hh * GDN_DV:(hh + 1) * GDN_DV]
        parts.append(oh * lax.rsqrt(jnp.mean(oh * oh, axis=-1, keepdims=True) + RMS_EPS) * nw_ref[...])
    ygdn = (jnp.concatenate(parts, axis=-1) * sz_ref[...].astype(F32)).astype(BF16)
    mix = _mm(yconv_ref[...], wtop_ref[...]) + _mm(ygdn, wbot_ref[...])
    out_ref[...] = _layer_norm(alpha * h_ref[...] + mix, g_ref[...], b_ref[...])


def _outproj_call(o, sz, yconv, h, wtop, wbot, nw, g, b, tb, alpha):
    n, d = h.shape
    full = lambda a: pl.BlockSpec(a.shape, lambda i: (0,) * a.ndim)
    blk = lambda w: pl.BlockSpec((tb, w), lambda i: (i, 0))
    return pl.pallas_call(
        functools.partial(_outproj_kernel, alpha=alpha),
        grid=(n // tb,),
        in_specs=[blk(GDN_WIDTH), blk(GDN_WIDTH), blk(CONV_WIDTH), blk(d), full(wtop), full(wbot), full(nw),
                  full(g), full(b)],
        out_specs=blk(d),
        out_shape=jax.ShapeDtypeStruct((n, d), F32),
        compiler_params=_params("parallel"),
    )(o, sz, yconv, h, wtop, wbot, nw, g, b)


def _sorting_network(n):
    pairs = []
    p = 1
    while p < n:
        k = p
        while k >= 1:
            for j in range(k % p, n - k, 2 * k):
                for i in range(min(k, n - j - k)):
                    if (i + j) // (2 * p) == (i + j + k) // (2 * p):
                        pairs.append((i + j, i + j + k))
            k //= 2
        p *= 2
    return pairs


def _top_values_sorted(s, n):
    groups = s.shape[0] // SUBLANES
    cols = [s[g * SUBLANES:(g + 1) * SUBLANES] for g in range(groups)]
    for a, b in _sorting_network(groups):
        cols[a], cols[b] = jnp.maximum(cols[a], cols[b]), jnp.minimum(cols[a], cols[b])
    tops = []
    for r in range(n):
        m = jnp.max(cols[0], axis=0, keepdims=True)
        tops.append(m)
        if r + 1 < n:
            hit = cols[0] >= m
            for k in range(min(groups, n - 1 - r)):
                below = cols[k + 1] if k + 1 < groups else NEG_INF
                cols[k] = jnp.where(hit, below, cols[k])
    return jnp.concatenate(tops, axis=0)


def _peer_kernel(xa_ref, xc_ref, wq_ref, k1_ref, k2_ref, u_ref, vt_ref, g_ref, b_ref, out_ref,
                 xb_ref, acc_ref, ht_ref, coef_ref, th_ref, aa_ref, s2_ref, bb_ref, *, alpha, ec, ne, total):
    s = pl.program_id(0)
    tb = xa_ref.shape[0]
    nk = PEER_TOPK + 1
    chunk_a = jnp.clip(s, 0, total - 1) % ne
    chunk_b = (s - 1) % ne
    chunk_c = (s - 2) % ne

    @pl.when(s == 0)
    def _():
        ht_ref[...] = jnp.zeros_like(ht_ref)
        coef_ref[...] = jnp.zeros_like(coef_ref)
        th_ref[...] = jnp.zeros_like(th_ref)
        aa_ref[...] = jnp.zeros_like(aa_ref)
        s2_ref[...] = jnp.zeros_like(s2_ref)
        bb_ref[...] = jnp.zeros_like(bb_ref)
        acc_ref[...] = jnp.zeros_like(acc_ref)

    @pl.when(jnp.logical_and(s < total, chunk_a == 0))
    def _():
        xb_ref[...] = xa_ref[...].astype(BF16)

    @pl.when(jnp.logical_and(jnp.logical_and(s >= 1, s <= total), chunk_b == 0))
    def _():
        xb = xb_ref[...]
        for hd in range(PEER_HEADS):
            lo = hd * 2 * PEER_HALF
            q = _mm(xb, wq_ref[:, lo:lo + 2 * PEER_HALF]).astype(BF16)
            s1_all = _mm_nt(k1_ref[...], q[:, :PEER_HALF])
            s2_all = _mm_nt(k2_ref[...], q[:, PEER_HALF:])
            for lt in range(tb // LANES):
                ls = slice(lt * LANES, (lt + 1) * LANES)
                s1 = s1_all[:, ls]
                s2 = s2_all[:, ls]
                t1 = _top_values_sorted(s1, nk)
                t2 = _top_values_sorted(s2, nk)
                cand = [t1[r:r + 1] + t2[:nk // (r + 1)] for r in range(nk)]
                fill = -sum(c.shape[0] for c in cand) % (SUBLANES * SUBLANES)
                cand = jnp.concatenate(cand + [jnp.full((fill, LANES), NEG_INF, F32)], axis=0)
                best = _top_values_sorted(cand, nk)
                tau = 0.5 * (best[PEER_TOPK - 1:PEER_TOPK] + best[PEER_TOPK:PEER_TOPK + 1])
                z = jnp.sum(jnp.exp(best[:PEER_TOPK] - best[0:1]), axis=0, keepdims=True)
                th = tau - s1
                aa = 0.5 * jnp.exp(s1 - t1[0:1]) / z
                for grp in range(PEER_NKEYS // SUBLANES):
                    th_ref[hd, grp, :, ls] = th[grp * SUBLANES:(grp + 1) * SUBLANES]
                    aa_ref[hd, grp, :, ls] = aa[grp * SUBLANES:(grp + 1) * SUBLANES]
                s2_ref[hd, :, ls] = s2
                bb_ref[hd, :, ls] = jnp.exp(s2 - t2[0:1])

    @pl.when(jnp.logical_and(s >= 2, chunk_c == 0))
    def _():
        acc_ref[...] = jnp.zeros_like(acc_ref)

    rows_per = PEER_NKEYS
    assert ec == SUBLANES * rows_per

    def stage_a(ht_w, ts, ms):
        ht_w[ms, ts] = _mm_nt(u_ref[ms, :], xb_ref[ts, :])

    def stage_b(ht_r, coef_w, ig, jh, lt):
        ls = slice(lt * LANES, (lt + 1) * LANES)
        js = slice(jh * (rows_per // PEER_JSPLIT), (jh + 1) * (rows_per // PEER_JSPLIT))
        iis = range(ig * PEER_IGROUP, (ig + 1) * PEER_IGROUP)
        gsum = [jnp.zeros((rows_per // PEER_JSPLIT, LANES), F32) for _ in iis]
        for hd in range(PEER_HEADS):
            s2 = s2_ref[hd, js, ls]
            bb = bb_ref[hd, js, ls]
            for k, ii in enumerate(iis):
                th = th_ref[hd, chunk_b, ii:ii + 1, ls]
                aa = aa_ref[hd, chunk_b, ii:ii + 1, ls]
                gsum[k] = gsum[k] + jnp.where(s2 >= th, bb, 0.0) * aa
        for k, ii in enumerate(iis):
            rs = slice(ii * rows_per + js.start, ii * rows_per + js.stop)
            ht = ht_r[rs, ls]
            act2 = ht * (1.0 + lax.erf(ht * np.float32(np.sqrt(0.5))))
            coef_w[rs, ls] = (gsum[k] * act2).astype(BF16)

    def stage_c(coef_r, ts, ms):
        acc_ref[ms, ts] += _mm(vt_ref[0, ms, :], coef_r[:, ts])

    assert acc_ref.shape[0] == ec
    mxu_pieces = [(slice(t0, t0 + PEER_SLICE), slice(m0, m0 + PEER_MROWS))
                  for t0 in range(0, tb, PEER_SLICE) for m0 in range(0, ec, PEER_MROWS)]
    valu_pieces = [(ig, jh, lt) for lt in range(tb // LANES) for ig in range(SUBLANES // PEER_IGROUP)
                   for jh in range(PEER_JSPLIT)]
    per = -(-len(valu_pieces) // len(mxu_pieces))

    def stages(ht_w, ht_r, coef_w, coef_r):
        for idx, (ts, ms) in enumerate(mxu_pieces):
            stage_a(ht_w, ts, ms)
            stage_c(coef_r, ts, ms)
            for piece in valu_pieces[idx * per:(idx + 1) * per]:
                stage_b(ht_r, coef_w, *piece)

    slot_a = s % 2
    slot_b = (s + 1) % 2
    stages(ht_ref.at[slot_a], ht_ref.at[slot_b], coef_ref.at[slot_b], coef_ref.at[slot_a])

    @pl.when(jnp.logical_and(s >= 2, chunk_c == ne - 1))
    def _():
        y = acc_ref[...].T
        out_ref[...] = _layer_norm(alpha * xc_ref[...] + y, g_ref[...], b_ref[...])


PEER_SLICE = 256
PEER_MROWS = 256
PEER_IGROUP = 4
PEER_JSPLIT = 2


def _peer_call(x, wq, k1, k2, u, vt, g, b, tb, ec, alpha):
    n, d = x.shape
    ne = u.shape[0] // ec
    total = (n // tb) * ne
    once = pl.Buffered(1)
    full = lambda a: pl.BlockSpec(a.shape, lambda s: (0,) * a.ndim, pipeline_mode=once)
    blk_a = lambda s: jnp.clip(s, 0, total - 1) // ne
    blk_c = lambda s: jnp.clip(s - 2, 0, total - 1) // ne
    head_scratch = pltpu.VMEM((PEER_HEADS, PEER_NKEYS, tb), F32)
    grouped_scratch = pltpu.VMEM((PEER_HEADS, PEER_NKEYS // SUBLANES, SUBLANES, tb), F32)
    return pl.pallas_call(
        functools.partial(_peer_kernel, alpha=alpha, ec=ec, ne=ne, total=total),
        grid=(total + 2,),
        in_specs=[pl.BlockSpec((tb, d), lambda s: (blk_a(s), 0), pipeline_mode=once),
                  pl.BlockSpec((tb, d), lambda s: (blk_c(s), 0), pipeline_mode=once),
                  full(wq), full(k1), full(k2),
                  pl.BlockSpec((ec, d), lambda s: (jnp.clip(s, 0, total - 1) % ne, 0)),
                  pl.BlockSpec((1, d, ec), lambda s: (jnp.clip(s - 2, 0, total - 1) % ne, 0, 0)),
                  full(g), full(b)],
        out_specs=pl.BlockSpec((tb, d), lambda s: (blk_c(s), 0)),
        out_shape=jax.ShapeDtypeStruct((n, d), F32),
        scratch_shapes=[pltpu.VMEM((tb, d), BF16), pltpu.VMEM((d, tb), F32),
                        pltpu.VMEM((2, ec, tb), F32), pltpu.VMEM((2, ec, tb), BF16),
                        grouped_scratch, grouped_scratch, head_scratch, head_scratch],
        compiler_params=_params("arbitrary"),
    )(x, x, wq, k1, k2, u, vt, g, b)


def kernel(x, meta_tokens, ln_in_g, ln_in_b, w_in, conv_w, gdn_conv_w, a_log, dt_bias, gdn_norm_w, w_out, ln1_g,
           ln1_b, peer_w_q, peer_k1, peer_k2, peer_u, peer_v, ln2_g, ln2_b):
    bsz, seq, d = x.shape
    depth = w_in.shape[0]
    alpha = (2.0 * depth) ** 0.25
    lp = SEQ_PAD + N_META + seq
    n = bsz * lp
    assert lp % GDN_CHUNK == 0

    tb_row = _pick_block(n, 768, LANES)
    tb_in = _pick_block(lp, 768, 2 * SUBLANES)
    chunks = 3 if (lp // GDN_CHUNK) % 3 == 0 else 1
    tb_peer = _pick_block(n, 768, PEER_SLICE)
    ec = SUBLANES * PEER_NKEYS
    n_exp = peer_u.shape[1]

    row = lambda a: a.reshape(1, -1).astype(F32)
    pad_lanes = lambda a: jnp.pad(a.reshape(1, -1).astype(F32), ((0, 0), (0, LANES - a.size)))

    meta = jnp.broadcast_to(meta_tokens[None].astype(x.dtype), (bsz, N_META, d))
    h = jnp.concatenate([jnp.zeros((bsz, SEQ_PAD, d), x.dtype), meta, x], axis=1).reshape(n, d)
    h = _ln_call(h, row(ln_in_g), row(ln_in_b), tb_row)

    c3 = 3 * CONV_WIDTH
    for l in range(depth):
        wl = w_in[l]
        wc = wl[:, :c3].astype(BF16)
        wqkv = wl[:, c3:c3 + QKV_COLS].astype(BF16)
        wz = wl[:, c3 + QKV_COLS:c3 + QKV_COLS + GDN_WIDTH].astype(BF16)
        wab = jnp.pad(wl[:, c3 + QKV_COLS + GDN_WIDTH:], ((0, 0), (0, LANES - 2 * GDN_HEADS))).astype(BF16)
        yconv, q, k, v, sz, gb = _inproj_call(
            h.reshape(bsz, lp, d), wc, wqkv, wz, wab, conv_w[l].astype(F32), gdn_conv_w[l].astype(F32),
            pad_lanes(a_log[l]), pad_lanes(dt_bias[l]), tb_in)
        flat = lambda a: a.reshape(n, a.shape[-1])
        wg, qd, kd, ug, attn, egl = _gdn_local_call(flat(q), flat(k), flat(v), flat(gb), chunks)
        seq = lambda a: a.reshape(bsz, lp, a.shape[-1])
        o = _gdn_scan_call(seq(wg), seq(qd), seq(kd), seq(ug), seq(attn),
                           egl.reshape(bsz, lp // GDN_CHUNK, SUBLANES, LANES), chunks)
        wo = w_out[l].astype(BF16)
        h = _outproj_call(o.reshape(n, GDN_WIDTH), sz.reshape(n, GDN_WIDTH), yconv.reshape(n, CONV_WIDTH), h,
                          wo[:CONV_WIDTH], wo[CONV_WIDTH:], row(gdn_norm_w[l]), row(ln1_g[l]), row(ln1_b[l]),
                          tb_row, alpha)
        ub = peer_u[l].astype(BF16)
        vt = peer_v[l].astype(BF16).reshape(n_exp // ec, ec, d).transpose(0, 2, 1)
        h = _peer_call(h, peer_w_q[l].astype(BF16), peer_k1[l].astype(BF16), peer_k2[l].astype(BF16), ub, vt,
                       row(ln2_g[l]), row(ln2_b[l]), tb_peer, ec, alpha)
    return h.reshape(bsz, lp, d)[:, SEQ_PAD + N_META:]
```

```python
import functools

import jax
import jax.numpy as jnp
import numpy as np
from jax import lax
from jax.experimental import pallas as pl
from jax.experimental.pallas import tpu as pltpu

F32 = jnp.float32
BF16 = jnp.bfloat16

N_META = 16
CONV_WIDTH = 512
CONV_K = 3
GDN_HEADS = 4
GDN_DK = 128
GDN_DV = 128
GDN_CONV_K = 4
GDN_CHUNK = 64
QKV_COLS = GDN_HEADS * (2 * GDN_DK + GDN_DV)
GDN_WIDTH = GDN_HEADS * GDN_DV
PEER_HEADS = 8
PEER_NKEYS = 128
PEER_HALF = 128
PEER_TOPK = 16
LN_EPS = 1e-5
RMS_EPS = 1e-6
SEQ_PAD = (-N_META) % GDN_CHUNK
LANES = 128
SUBLANES = 8
VMEM_LIMIT = 56 * 1024 * 1024
NEG_INF = float("-inf")


def _mm(a, b, precision=None):
    return jnp.dot(a, b, preferred_element_type=F32, precision=precision)


def _mm_nt(a, b, precision=None):
    return lax.dot_general(a, b, (((1,), (1,)), ((), ())), preferred_element_type=F32, precision=precision)


def _mm_tn(a, b, precision=None):
    return lax.dot_general(a, b, (((0,), (0,)), ((), ())), preferred_element_type=F32, precision=precision)


def _layer_norm(x, g, b):
    mu = jnp.mean(x, axis=-1, keepdims=True)
    xc = x - mu
    var = jnp.mean(xc * xc, axis=-1, keepdims=True)
    return xc * lax.rsqrt(var + LN_EPS) * g + b


def _silu(x):
    return x * jax.nn.sigmoid(x)


def _pick_block(n, target, mult):
    best = None
    for d in range(mult, min(n, target) + 1, mult):
        if n % d == 0:
            best = d
    assert best is not None, (n, target, mult)
    return best


def _params(*sem):
    return pltpu.CompilerParams(dimension_semantics=sem, vmem_limit_bytes=VMEM_LIMIT)


def _ln_kernel(x_ref, g_ref, b_ref, o_ref):
    o_ref[...] = _layer_norm(x_ref[...], g_ref[...], b_ref[...])


def _ln_call(x2d, g, b, tb):
    n, d = x2d.shape
    return pl.pallas_call(
        _ln_kernel,
        grid=(n // tb,),
        in_specs=[pl.BlockSpec((tb, d), lambda i: (i, 0)),
                  pl.BlockSpec((1, d), lambda i: (0, 0)),
                  pl.BlockSpec((1, d), lambda i: (0, 0))],
        out_specs=pl.BlockSpec((tb, d), lambda i: (i, 0)),
        out_shape=jax.ShapeDtypeStruct((n, d), F32),
        compiler_params=_params("parallel"),
    )(x2d, g, b)


def _causal_conv(u, prev, w):
    k = w.shape[0]

    def taps(a):
        acc = a * w[k - 1:k]
        for s in range(1, k):
            acc = acc + pltpu.roll(a, s, 0) * w[k - 1 - s:k - s]
        return acc

    body = taps(u)
    head = taps(jnp.concatenate([prev, u[:SUBLANES]], axis=0))[SUBLANES:]
    return jnp.concatenate([head, body[SUBLANES:]], axis=0)


def _inproj_kernel(h_ref, wc_ref, wqkv_ref, wz_ref, wab_ref, cw_ref, gcw_ref, alog_ref, dtb_ref,
                   yconv_ref, q_ref, k_ref, v_ref, sz_ref, gb_ref, carry_c, carry_qkv):
    j = pl.program_id(1)
    tb = h_ref.shape[1]

    @pl.when(j == 0)
    def _():
        carry_c[...] = jnp.zeros_like(carry_c)
        carry_qkv[...] = jnp.zeros_like(carry_qkv)

    xb = h_ref[0].astype(BF16)
    row = j * tb + lax.broadcasted_iota(jnp.int32, (tb, 1), 0)
    valid = row >= SEQ_PAD

    p = _mm(xb, wc_ref[...])
    u = jnp.where(valid, p[:, CONV_WIDTH:2 * CONV_WIDTH] * p[:, 2 * CONV_WIDTH:], 0.0)
    yconv_ref[0] = (p[:, :CONV_WIDTH] * _causal_conv(u, carry_c[...], cw_ref[...])).astype(BF16)
    carry_c[...] = u[tb - SUBLANES:]

    p = jnp.where(valid, _mm(xb, wqkv_ref[...]), 0.0)
    s = _silu(_causal_conv(p, carry_qkv[...], gcw_ref[...]))
    carry_qkv[...] = p[tb - SUBLANES:]
    for hh in range(GDN_HEADS):
        lo = hh * GDN_DK
        qh = s[:, lo:lo + GDN_DK]
        kh = s[:, GDN_WIDTH + lo:GDN_WIDTH + lo + GDN_DK]
        q_ref[0, :, lo:lo + GDN_DK] = qh * lax.rsqrt(jnp.sum(qh * qh, axis=-1, keepdims=True) + RMS_EPS)
        k_ref[0, :, lo:lo + GDN_DK] = kh * lax.rsqrt(jnp.sum(kh * kh, axis=-1, keepdims=True) + RMS_EPS)
    v_ref[0] = s[:, 2 * GDN_WIDTH:]

    sz_ref[0] = _silu(_mm(xb, wz_ref[...])).astype(BF16)

    ab = _mm(xb, wab_ref[...])
    t = ab + dtb_ref[...]
    softplus = jnp.maximum(t, 0.0) + jnp.log1p(jnp.exp(-jnp.abs(t)))
    g = -jnp.exp(alog_ref[...]) * softplus
    lane = lax.broadcasted_iota(jnp.int32, (1, LANES), 1)
    gb = jnp.where(lane < GDN_HEADS, g, jax.nn.sigmoid(ab))
    gb_ref[0] = jnp.where(valid, gb, 0.0)


def _inproj_call(h, wc, wqkv, wz, wab, cw, gcw, alog, dtb, tb):
    b, lp, d = h.shape
    full = lambda a: pl.BlockSpec(a.shape, lambda bi, j: (0,) * a.ndim)
    blk = lambda w: pl.BlockSpec((1, tb, w), lambda bi, j: (bi, j, 0))
    out = lambda w, dt: jax.ShapeDtypeStruct((b, lp, w), dt)
    return pl.pallas_call(
        _inproj_kernel,
        grid=(b, lp // tb),
        in_specs=[blk(d), full(wc), full(wqkv), full(wz), full(wab), full(cw), full(gcw), full(alog), full(dtb)],
        out_specs=[blk(CONV_WIDTH), blk(GDN_WIDTH), blk(GDN_WIDTH), blk(GDN_WIDTH), blk(GDN_WIDTH), blk(LANES)],
        out_shape=[out(CONV_WIDTH, BF16), out(GDN_WIDTH, F32), out(GDN_WIDTH, F32), out(GDN_WIDTH, F32),
                   out(GDN_WIDTH, BF16), out(LANES, F32)],
        scratch_shapes=[pltpu.VMEM((SUBLANES, CONV_WIDTH), F32), pltpu.VMEM((SUBLANES, QKV_COLS), F32)],
        compiler_params=_params("parallel", "arbitrary"),
    )(h, wc, wqkv, wz, wab, cw, gcw, alog, dtb)


GDN_BASE = 16


def _gdn_local_kernel(q_ref, k_ref, v_ref, gb_ref, w_ref, qd_ref, kd_ref, u_ref, attn_ref, egl_ref, *, chunks):
    c = GDN_CHUNK
    hi = lax.Precision.HIGHEST
    rows_all = chunks * c

    ri = lax.broadcasted_iota(jnp.int32, (c, c), 0)
    ci = lax.broadcasted_iota(jnp.int32, (c, c), 1)
    causal = ri >= ci
    strict = ri > ci
    eye = (ri == ci).astype(F32)
    base = jnp.logical_and(strict, (ri // GDN_BASE) == (ci // GDN_BASE))
    levels = [jnp.logical_and(jnp.logical_and((ri >> (l + 1)) == (ci >> (l + 1)), ((ri >> l) & 1) == 1),
                              ((ci >> l) & 1) == 0) for l in range(GDN_BASE.bit_length() - 1, c.bit_length() - 1)]
    sel = (lax.broadcasted_iota(jnp.int32, (SUBLANES, LANES), 0)
           == lax.broadcasted_iota(jnp.int32, (SUBLANES, LANES), 1)).astype(F32)

    gb_all = gb_ref[...]
    pos = lax.broadcasted_iota(jnp.int32, (rows_all, 1), 0) % c
    gcum_all = gb_all
    for sh in (1 << k for k in range(c.bit_length() - 1)):
        gcum_all = gcum_all + jnp.where(pos >= sh, pltpu.roll(gcum_all, sh, 0), 0.0)
    gcum_t_all = _mm_nt(sel, gcum_all, hi)

    for cc in range(chunks):
        last = cc * c + c - 1
        egl_ref[cc] = jnp.exp(jnp.broadcast_to(gcum_t_all[:, last:last + 1], (SUBLANES, LANES)))

    items = [(cc, hh) for cc in range(chunks) for hh in range(GDN_HEADS)]
    each = lambda fn, *lists: [fn(*args) for args in zip(*lists)]
    rows_of = [slice(cc * c, (cc + 1) * c) for cc, _ in items]
    cols_of = [slice(hh * GDN_DK, (hh + 1) * GDN_DK) for _, hh in items]

    kh = each(lambda r, l: k_ref[r, l], rows_of, cols_of)
    khb = each(lambda x: x.astype(BF16), kh)
    beta = [gb_all[r, GDN_HEADS + hh:GDN_HEADS + hh + 1] for r, (_, hh) in zip(rows_of, items)]
    gcol = [gcum_all[r, hh:hh + 1] for r, (_, hh) in zip(rows_of, items)]
    grow = [gcum_t_all[hh:hh + 1, r] for r, (_, hh) in zip(rows_of, items)]
    decay = each(lambda gc_, gr_: jnp.where(causal, jnp.exp(jnp.where(causal, gc_ - gr_, 0.0)), 0.0), gcol, grow)
    kbeta = each(lambda x, b_: x * b_, kh, beta)
    a = each(lambda kb_, k_, d_: jnp.where(strict, _mm_nt(kb_.astype(BF16), k_) * d_, 0.0), kbeta, khb, decay)
    n1 = each(lambda a_: jnp.where(base, -a_, 0.0), a)
    n1b = each(lambda x: x.astype(BF16), n1)
    n2 = each(lambda x: _mm(x, x), n1b)
    npow = each(lambda x: x.astype(BF16), n2)
    tinv = each(lambda n1_, n2_, n1b_, n2b_: eye + n1_ + n2_ + _mm(n1b_, n2b_), n1, n2, n1b, npow)
    for _ in range(GDN_BASE.bit_length() - 3):
        npow = each(lambda x: _mm(x, x).astype(BF16), npow)
        tinv = each(lambda t_, p_: t_ + _mm(t_.astype(BF16), p_), tinv, npow)
    for m in levels:
        tb16 = each(lambda t_: t_.astype(BF16), tinv)
        mt = each(lambda a_, t_: _mm(jnp.where(m, a_, 0.0).astype(BF16), t_).astype(BF16), a, tb16)
        tinv = each(lambda t_, tb_, mt_: t_ - _mm(tb_, mt_), tinv, tb16, mt)
    tb16 = each(lambda t_: t_.astype(BF16), tinv)
    egc = each(jnp.exp, gcol)
    vh = each(lambda r, l: v_ref[r, l], rows_of, cols_of)
    rhs = each(lambda v_, b_, kb_, e_: jnp.concatenate([v_ * b_, kb_ * e_], axis=1).astype(BF16),
               vh, beta, kbeta, egc)
    uw = each(_mm, tb16, rhs)
    qh = each(lambda r, l: q_ref[r, l] * (GDN_DK ** -0.5), rows_of, cols_of)
    attn = each(lambda q_, k_, d_: jnp.where(causal, _mm_nt(q_.astype(BF16), k_) * d_, 0.0).astype(BF16),
                qh, khb, decay)
    for idx, (cc, hh) in enumerate(items):
        r, l = rows_of[idx], cols_of[idx]
        u_ref[r, l] = uw[idx][:, :GDN_DV]
        w_ref[r, l] = uw[idx][:, GDN_DV:].astype(BF16)
        attn_ref[r, hh * c:(hh + 1) * c] = attn[idx]
        glast = gcol[idx][c - 1:c, :]
        kd_ref[r, l] = (kh[idx] * jnp.exp(glast - gcol[idx])).astype(BF16)
        qd_ref[r, l] = (qh[idx] * egc[idx]).astype(BF16)


def _gdn_local_call(q, k, v, gb, chunks):
    n, w = q.shape
    tb = chunks * GDN_CHUNK
    blk = lambda width: pl.BlockSpec((tb, width), lambda i: (i, 0))
    out = lambda width, dt: jax.ShapeDtypeStruct((n, width), dt)
    return pl.pallas_call(
        functools.partial(_gdn_local_kernel, chunks=chunks),
        grid=(n // tb,),
        in_specs=[blk(w), blk(w), blk(w), blk(LANES)],
        out_specs=[blk(w), blk(w), blk(w), blk(w), blk(GDN_HEADS * GDN_CHUNK),
                   pl.BlockSpec((chunks, SUBLANES, LANES), lambda i: (i, 0, 0))],
        out_shape=[out(w, BF16), out(w, BF16), out(w, BF16), out(w, F32), out(GDN_HEADS * GDN_CHUNK, BF16),
                   jax.ShapeDtypeStruct((n // GDN_CHUNK, SUBLANES, LANES), F32)],
        compiler_params=_params("parallel"),
    )(q, k, v, gb)


def _gdn_scan_kernel(w_ref, qd_ref, kd_ref, u_ref, attn_ref, egl_ref, o_ref, state, *, chunks):
    c = GDN_CHUNK
    bsz = w_ref.shape[0]

    @pl.when(pl.program_id(0) == 0)
    def _():
        state[...] = jnp.zeros_like(state)

    items = [(b, hh) for b in range(bsz) for hh in range(GDN_HEADS)]
    each = lambda fn, *lists: [fn(*args) for args in zip(*lists)]
    cols_of = [slice(hh * GDN_DK, (hh + 1) * GDN_DK) for _, hh in items]
    sh = [state[idx] for idx in range(len(items))]
    for cc in range(chunks):
        rows = slice(cc * c, (cc + 1) * c)
        sb = each(lambda s_: s_.astype(BF16), sh)
        wq = [jnp.concatenate([w_ref[b, rows, l], qd_ref[b, rows, l]], axis=0) for (b, _), l in zip(items, cols_of)]
        r = each(_mm, wq, sb)
        vnb = [(u_ref[b, rows, l] - r_[:c]).astype(BF16) for (b, _), l, r_ in zip(items, cols_of, r)]
        for (b, hh), l, r_, v_ in zip(items, cols_of, r, vnb):
            o_ref[b, rows, l] = r_[c:] + _mm(attn_ref[b, rows, hh * c:(hh + 1) * c], v_)
        sh = [s_ * egl_ref[b, cc, hh:hh + 1, :] + _mm_tn(kd_ref[b, rows, l], v_)
              for (b, hh), l, s_, v_ in zip(items, cols_of, sh, vnb)]
    for idx, s_ in enumerate(sh):
        state[idx] = s_


def _gdn_scan_call(w, qd, kd, u, attn, egl, chunks):
    b, lp, width = u.shape
    tb = chunks * GDN_CHUNK
    blk = lambda wd: pl.BlockSpec((b, tb, wd), lambda j: (0, j, 0))
    return pl.pallas_call(
        functools.partial(_gdn_scan_kernel, chunks=chunks),
        grid=(lp // tb,),
        in_specs=[blk(width), blk(width), blk(width), blk(width), blk(GDN_HEADS * GDN_CHUNK),
                  pl.BlockSpec((b, chunks, SUBLANES, LANES), lambda j: (0, j, 0, 0))],
        out_specs=blk(width),
        out_shape=jax.ShapeDtypeStruct((b, lp, width), F32),
        scratch_shapes=[pltpu.VMEM((b * GDN_HEADS, GDN_DK, GDN_DV), F32)],
        compiler_params=_params("arbitrary"),
    )(w, qd, kd, u, attn, egl)


def _outproj_kernel(o_ref, sz_ref, yconv_ref, h_ref, wtop_ref, wbot_ref, nw_ref, g_ref, b_ref, out_ref, *, alpha):
    o = o_ref[...]
    parts = []
    for hh in range(GDN_HEADS):
        oh = o[:, hh * GDN_DV:(hh + 1) * GDN_DV]
        parts.append(oh * lax.rsqrt(jnp.mean(oh * oh, axis=-1, keepdims=True) + RMS_EPS) * nw_ref[...])
    ygdn = (jnp.concatenate(parts, axis=-1) * sz_ref[...].astype(F32)).astype(BF16)
    mix = _mm(yconv_ref[...], wtop_ref[...]) + _mm(ygdn, wbot_ref[...])
    out_ref[...] = _layer_norm(alpha * h_ref[...] + mix, g_ref[...], b_ref[...])


def _outproj_call(o, sz, yconv, h, wtop, wbot, nw, g, b, tb, alpha):
    n, d = h.shape
    full = lambda a: pl.BlockSpec(a.shape, lambda i: (0,) * a.ndim)
    blk = lambda w: pl.BlockSpec((tb, w), lambda i: (i, 0))
    return pl.pallas_call(
        functools.partial(_outproj_kernel, alpha=alpha),
        grid=(n // tb,),
        in_specs=[blk(GDN_WIDTH), blk(GDN_WIDTH), blk(CONV_WIDTH), blk(d), full(wtop), full(wbot), full(nw),
                  full(g), full(b)],
        out_specs=blk(d),
        out_shape=jax.ShapeDtypeStruct((n, d), F32),
        compiler_params=_params("parallel"),
    )(o, sz, yconv, h, wtop, wbot, nw, g, b)


def _sorting_network(n):
    pairs = []
    p = 1
    while p < n:
        k = p
        while k >= 1:
            for j in range(k % p, n - k, 2 * k):
                for i in range(min(k, n - j - k)):
                    if (i + j) // (2 * p) == (i + j + k) // (2 * p):
                        pairs.append((i + j, i + j + k))
            k //= 2
        p *= 2
    return pairs


def _top_values_sorted(s, n):
    groups = s.shape[0] // SUBLANES
    cols = [s[g * SUBLANES:(g + 1) * SUBLANES] for g in range(groups)]
    for a, b in _sorting_network(groups):
        cols[a], cols[b] = jnp.maximum(cols[a], cols[b]), jnp.minimum(cols[a], cols[b])
    tops = []
    for r in range(n):
        m = jnp.max(cols[0], axis=0, keepdims=True)
        tops.append(m)
        if r + 1 < n:
            hit = cols[0] >= m
            for k in range(min(groups, n - 1 - r)):
                below = cols[k + 1] if k + 1 < groups else NEG_INF
                cols[k] = jnp.where(hit, below, cols[k])
    return jnp.concatenate(tops, axis=0)


def _peer_kernel(xa_ref, xc_ref, wq_ref, k1_ref, k2_ref, u_ref, vt_ref, g_ref, b_ref, out_ref,
                 xb_ref, acc_ref, ht_ref, coef_ref, th_ref, aa_ref, s2_ref, bb_ref, *, alpha, ec, ne, total):
    s = pl.program_id(0)
    tb = xa_ref.shape[0]
    nk = PEER_TOPK + 1
    chunk_a = jnp.clip(s, 0, total - 1) % ne
    chunk_b = (s - 1) % ne
    chunk_c = (s - 2) % ne

    @pl.when(s == 0)
    def _():
        ht_ref[...] = jnp.zeros_like(ht_ref)
        coef_ref[...] = jnp.zeros_like(coef_ref)
        th_ref[...] = jnp.zeros_like(th_ref)
        aa_ref[...] = jnp.zeros_like(aa_ref)
        s2_ref[...] = jnp.zeros_like(s2_ref)
        bb_ref[...] = jnp.zeros_like(bb_ref)
        acc_ref[...] = jnp.zeros_like(acc_ref)

    @pl.when(jnp.logical_and(s < total, chunk_a == 0))
    def _():
        xb_ref[...] = xa_ref[...].astype(BF16)

    @pl.when(jnp.logical_and(jnp.logical_and(s >= 1, s <= total), chunk_b == 0))
    def _():
        xb = xb_ref[...]
        for hd in range(PEER_HEADS):
            lo = hd * 2 * PEER_HALF
            q = _mm(xb, wq_ref[:, lo:lo + 2 * PEER_HALF]).astype(BF16)
            s1_all = _mm_nt(k1_ref[...], q[:, :PEER_HALF])
            s2_all = _mm_nt(k2_ref[...], q[:, PEER_HALF:])
            for lt in range(tb // LANES):
                ls = slice(lt * LANES, (lt + 1) * LANES)
                s1 = s1_all[:, ls]
                s2 = s2_all[:, ls]
                t1 = _top_values_sorted(s1, nk)
                t2 = _top_values_sorted(s2, nk)
                cand = [t1[r:r + 1] + t2[:nk // (r + 1)] for r in range(nk)]
                fill = -sum(c.shape[0] for c in cand) % (SUBLANES * SUBLANES)
                cand = jnp.concatenate(cand + [jnp.full((fill, LANES), NEG_INF, F32)], axis=0)
                best = _top_values_sorted(cand, nk)
                tau = 0.5 * (best[PEER_TOPK - 1:PEER_TOPK] + best[PEER_TOPK:PEER_TOPK + 1])
                z = jnp.sum(jnp.exp(best[:PEER_TOPK] - best[0:1]), axis=0, keepdims=True)
                th = tau - s1
                aa = 0.5 * jnp.exp(s1 - t1[0:1]) / z
                for grp in range(PEER_NKEYS // SUBLANES):
                    th_ref[hd, grp, :, ls] = th[grp * SUBLANES:(grp + 1) * SUBLANES]
                    aa_ref[hd, grp, :, ls] = aa[grp * SUBLANES:(grp + 1) * SUBLANES]
                s2_ref[hd, :, ls] = s2
                bb_ref[hd, :, ls] = jnp.exp(s2 - t2[0:1])

    @pl.when(jnp.logical_and(s >= 2, chunk_c == 0))
    def _():
        acc_ref[...] = jnp.zeros_like(acc_ref)

    rows_per = PEER_NKEYS
    assert ec == SUBLANES * rows_per

    def stage_a(ht_w, ts, ms):
        ht_w[ms, ts] = _mm_nt(u_ref[ms, :], xb_ref[ts, :])

    def stage_b(ht_r, coef_w, ig, jh, lt):
        ls = slice(lt * LANES, (lt + 1) * LANES)
        js = slice(jh * (rows_per // PEER_JSPLIT), (jh + 1) * (rows_per // PEER_JSPLIT))
        iis = range(ig * PEER_IGROUP, (ig + 1) * PEER_IGROUP)
        gsum = [jnp.zeros((rows_per // PEER_JSPLIT, LANES), F32) for _ in iis]
        for hd in range(PEER_HEADS):
            s2 = s2_ref[hd, js, ls]
            bb = bb_ref[hd, js, ls]
            for k, ii in enumerate(iis):
                th = th_ref[hd, chunk_b, ii:ii + 1, ls]
                aa = aa_ref[hd, chunk_b, ii:ii + 1, ls]
                gsum[k] = gsum[k] + jnp.where(s2 >= th, bb, 0.0) * aa
        for k, ii in enumerate(iis):
            rs = slice(ii * rows_per + js.start, ii * rows_per + js.stop)
            ht = ht_r[rs, ls]
            act2 = ht * (1.0 + lax.erf(ht * np.float32(np.sqrt(0.5))))
            coef_w[rs, ls] = (gsum[k] * act2).astype(BF16)

    def stage_c(coef_r, ts, ms):
        acc_ref[ms, ts] += _mm(vt_ref[0, ms, :], coef_r[:, ts])

    assert acc_ref.shape[0] == ec
    mxu_pieces = [(slice(t0, t0 + PEER_SLICE), slice(m0, m0 + PEER_MROWS))
                  for t0 in range(0, tb, PEER_SLICE) for m0 in range(0, ec, PEER_MROWS)]
    valu_pieces = [(ig, jh, lt) for lt in range(tb // LANES) for ig in range(SUBLANES // PEER_IGROUP)
                   for jh in range(PEER_JSPLIT)]
    per = -(-len(valu_pieces) // len(mxu_pieces))

    def stages(ht_w, ht_r, coef_w, coef_r):
        for idx, (ts, ms) in enumerate(mxu_pieces):
            stage_a(ht_w, ts, ms)
            stage_c(coef_r, ts, ms)
            for piece in valu_pieces[idx * per:(idx + 1) * per]:
                stage_b(ht_r, coef_w, *piece)

    slot_a = s % 2
    slot_b = (s + 1) % 2
    stages(ht_ref.at[slot_a], ht_ref.at[slot_b], coef_ref.at[slot_b], coef_ref.at[slot_a])

    @pl.when(jnp.logical_and(s >= 2, chunk_c == ne - 1))
    def _():
        y = acc_ref[...].T
        out_ref[...] = _layer_norm(alpha * xc_ref[...] + y, g_ref[...], b_ref[...])


PEER_SLICE = 256
PEER_MROWS = 256
PEER_IGROUP = 4
PEER_JSPLIT = 2


def _peer_call(x, wq, k1, k2, u, vt, g, b, tb, ec, alpha):
    n, d = x.shape
    ne = u.shape[0] // ec
    total = (n // tb) * ne
    once = pl.Buffered(1)
    full = lambda a: pl.BlockSpec(a.shape, lambda s: (0,) * a.ndim, pipeline_mode=once)
    blk_a = lambda s: jnp.clip(s, 0, total - 1) // ne
    blk_c = lambda s: jnp.clip(s - 2, 0, total - 1) // ne
    head_scratch = pltpu.VMEM((PEER_HEADS, PEER_NKEYS, tb), F32)
    grouped_scratch = pltpu.VMEM((PEER_HEADS, PEER_NKEYS // SUBLANES, SUBLANES, tb), F32)
    return pl.pallas_call(
        functools.partial(_peer_kernel, alpha=alpha, ec=ec, ne=ne, total=total),
        grid=(total + 2,),
        in_specs=[pl.BlockSpec((tb, d), lambda s: (blk_a(s), 0), pipeline_mode=once),
                  pl.BlockSpec((tb, d), lambda s: (blk_c(s), 0), pipeline_mode=once),
                  full(wq), full(k1), full(k2),
                  pl.BlockSpec((ec, d), lambda s: (jnp.clip(s, 0, total - 1) % ne, 0)),
                  pl.BlockSpec((1, d, ec), lambda s: (jnp.clip(s - 2, 0, total - 1) % ne, 0, 0)),
                  full(g), full(b)],
        out_specs=pl.BlockSpec((tb, d), lambda s: (blk_c(s), 0)),
        out_shape=jax.ShapeDtypeStruct((n, d), F32),
        scratch_shapes=[pltpu.VMEM((tb, d), BF16), pltpu.VMEM((d, tb), F32),
                        pltpu.VMEM((2, ec, tb), F32), pltpu.VMEM((2, ec, tb), BF16),
                        grouped_scratch, grouped_scratch, head_scratch, head_scratch],
        compiler_params=_params("arbitrary"),
    )(x, x, wq, k1, k2, u, vt, g, b)


def kernel(x, meta_tokens, ln_in_g, ln_in_b, w_in, conv_w, gdn_conv_w, a_log, dt_bias, gdn_norm_w, w_out, ln1_g,
           ln1_b, peer_w_q, peer_k1, peer_k2, peer_u, peer_v, ln2_g, ln2_b):
    bsz, seq, d = x.shape
    depth = w_in.shape[0]
    alpha = (2.0 * depth) ** 0.25
    lp = SEQ_PAD + N_META + seq
    n = bsz * lp
    assert lp % GDN_CHUNK == 0

    tb_row = _pick_block(n, 768, LANES)
    tb_in = _pick_block(lp, 768, 2 * SUBLANES)
    chunks = 3 if (lp // GDN_CHUNK) % 3 == 0 else 1
    tb_peer = _pick_block(n, 768, PEER_SLICE)
    ec = SUBLANES * PEER_NKEYS
    n_exp = peer_u.shape[1]

    row = lambda a: a.reshape(1, -1).astype(F32)
    pad_lanes = lambda a: jnp.pad(a.reshape(1, -1).astype(F32), ((0, 0), (0, LANES - a.size)))

    meta = jnp.broadcast_to(meta_tokens[None].astype(x.dtype), (bsz, N_META, d))
    h = jnp.concatenate([jnp.zeros((bsz, SEQ_PAD, d), x.dtype), meta, x], axis=1).reshape(n, d)
    h = _ln_call(h, row(ln_in_g), row(ln_in_b), tb_row)

    c3 = 3 * CONV_WIDTH
    for l in range(depth):
        wl = w_in[l]
        wc = wl[:, :c3].astype(BF16)
        wqkv = wl[:, c3:c3 + QKV_COLS].astype(BF16)
        wz = wl[:, c3 + QKV_COLS:c3 + QKV_COLS + GDN_WIDTH].astype(BF16)
        wab = jnp.pad(wl[:, c3 + QKV_COLS + GDN_WIDTH:], ((0, 0), (0, LANES - 2 * GDN_HEADS))).astype(BF16)
        yconv, q, k, v, sz, gb = _inproj_call(
            h.reshape(bsz, lp, d), wc, wqkv, wz, wab, conv_w[l].astype(F32), gdn_conv_w[l].astype(F32),
            pad_lanes(a_log[l]), pad_lanes(dt_bias[l]), tb_in)
        flat = lambda a: a.reshape(n, a.shape[-1])
        wg, qd, kd, ug, attn, egl = _gdn_local_call(flat(q), flat(k), flat(v), flat(gb), chunks)
        seq = lambda a: a.reshape(bsz, lp, a.shape[-1])
        o = _gdn_scan_call(seq(wg), seq(qd), seq(kd), seq(ug), seq(attn),
                           egl.reshape(bsz, lp // GDN_CHUNK, SUBLANES, LANES), chunks)
        wo = w_out[l].astype(BF16)
        h = _outproj_call(o.reshape(n, GDN_WIDTH), sz.reshape(n, GDN_WIDTH), yconv.reshape(n, CONV_WIDTH), h,
                          wo[:CONV_WIDTH], wo[CONV_WIDTH:], row(gdn_norm_w[l]), row(ln1_g[l]), row(ln1_b[l]),
                          tb_row, alpha)
        ub = peer_u[l].astype(BF16)
        vt = peer_v[l].astype(BF16).reshape(n_exp // ec, ec, d).transpose(0, 2, 1)
        h = _peer_call(h, peer_w_q[l].astype(BF16), peer_k1[l].astype(BF16), peer_k2[l].astype(BF16), ub, vt,
                       row(ln2_g[l]), row(ln2_b[l]), tb_peer, ec, alpha)
    return h.reshape(bsz, lp, d)[:, SEQ_PAD + N_META:]
```

```python
import functools

import jax
import jax.numpy as jnp
import numpy as np
from jax import lax
from jax.experimental import pallas as pl
from jax.experimental.pallas import tpu as pltpu

F32 = jnp.float32
BF16 = jnp.bfloat16

N_META = 16
CONV_WIDTH = 512
CONV_K = 3
GDN_HEADS = 4
GDN_DK = 128
GDN_DV = 128
GDN_CONV_K = 4
GDN_CHUNK = 64
QKV_COLS = GDN_HEADS * (2 * GDN_DK + GDN_DV)
GDN_WIDTH = GDN_HEADS * GDN_DV
PEER_HEADS = 8
PEER_NKEYS = 128
PEER_HALF = 128
PEER_TOPK = 16
LN_EPS = 1e-5
RMS_EPS = 1e-6
SEQ_PAD = (-N_META) % GDN_CHUNK
LANES = 128
SUBLANES = 8
VMEM_LIMIT = 56 * 1024 * 1024
NEG_INF = float("-inf")


def _mm(a, b, precision=None):
    return jnp.dot(a, b, preferred_element_type=F32, precision=precision)


def _mm_nt(a, b, precision=None):
    return lax.dot_general(a, b, (((1,), (1,)), ((), ())), preferred_element_type=F32, precision=precision)


def _mm_tn(a, b, precision=None):
    return lax.dot_general(a, b, (((0,), (0,)), ((), ())), preferred_element_type=F32, precision=precision)


def _layer_norm(x, g, b):
    mu = jnp.mean(x, axis=-1, keepdims=True)
    xc = x - mu
    var = jnp.mean(xc * xc, axis=-1, keepdims=True)
    return xc * lax.rsqrt(var + LN_EPS) * g + b


def _silu(x):
    return x * jax.nn.sigmoid(x)


def _pick_block(n, target, mult):
    best = None
    for d in range(mult, min(n, target) + 1, mult):
        if n % d == 0:
            best = d
    assert best is not None, (n, target, mult)
    return best


def _params(*sem):
    return pltpu.CompilerParams(dimension_semantics=sem, vmem_limit_bytes=VMEM_LIMIT)


def _ln_kernel(x_ref, g_ref, b_ref, o_ref):
    o_ref[...] = _layer_norm(x_ref[...], g_ref[...], b_ref[...])


def _ln_call(x2d, g, b, tb):
    n, d = x2d.shape
    return pl.pallas_call(
        _ln_kernel,
        grid=(n // tb,),
        in_specs=[pl.BlockSpec((tb, d), lambda i: (i, 0)),
                  pl.BlockSpec((1, d), lambda i: (0, 0)),
                  pl.BlockSpec((1, d), lambda i: (0, 0))],
        out_specs=pl.BlockSpec((tb, d), lambda i: (i, 0)),
        out_shape=jax.ShapeDtypeStruct((n, d), F32),
        compiler_params=_params("parallel"),
    )(x2d, g, b)


def _causal_conv(u, prev, w):
    k = w.shape[0]

    def taps(a):
        acc = a * w[k - 1:k]
        for s in range(1, k):
            acc = acc + pltpu.roll(a, s, 0) * w[k - 1 - s:k - s]
        return acc

    body = taps(u)
    head = taps(jnp.concatenate([prev, u[:SUBLANES]], axis=0))[SUBLANES:]
    return jnp.concatenate([head, body[SUBLANES:]], axis=0)


def _inproj_kernel(h_ref, wc_ref, wqkv_ref, wz_ref, wab_ref, cw_ref, gcw_ref, alog_ref, dtb_ref,
                   yconv_ref, q_ref, k_ref, v_ref, sz_ref, gb_ref, carry_c, carry_qkv):
    j = pl.program_id(1)
    tb = h_ref.shape[1]

    @pl.when(j == 0)
    def _():
        carry_c[...] = jnp.zeros_like(carry_c)
        carry_qkv[...] = jnp.zeros_like(carry_qkv)

    xb = h_ref[0].astype(BF16)
    row = j * tb + lax.broadcasted_iota(jnp.int32, (tb, 1), 0)
    valid = row >= SEQ_PAD

    p = _mm(xb, wc_ref[...])
    u = jnp.where(valid, p[:, CONV_WIDTH:2 * CONV_WIDTH] * p[:, 2 * CONV_WIDTH:], 0.0)
    yconv_ref[0] = (p[:, :CONV_WIDTH] * _causal_conv(u, carry_c[...], cw_ref[...])).astype(BF16)
    carry_c[...] = u[tb - SUBLANES:]

    p = jnp.where(valid, _mm(xb, wqkv_ref[...]), 0.0)
    s = _silu(_causal_conv(p, carry_qkv[...], gcw_ref[...]))
    carry_qkv[...] = p[tb - SUBLANES:]
    for hh in range(GDN_HEADS):
        lo = hh * GDN_DK
        qh = s[:, lo:lo + GDN_DK]
        kh = s[:, GDN_WIDTH + lo:GDN_WIDTH + lo + GDN_DK]
        q_ref[0, :, lo:lo + GDN_DK] = qh * lax.rsqrt(jnp.sum(qh * qh, axis=-1, keepdims=True) + RMS_EPS)
        k_ref[0, :, lo:lo + GDN_DK] = kh * lax.rsqrt(jnp.sum(kh * kh, axis=-1, keepdims=True) + RMS_EPS)
    v_ref[0] = s[:, 2 * GDN_WIDTH:]

    sz_ref[0] = _silu(_mm(xb, wz_ref[...])).astype(BF16)

    ab = _mm(xb, wab_ref[...])
    t = ab + dtb_ref[...]
    softplus = jnp.maximum(t, 0.0) + jnp.log1p(jnp.exp(-jnp.abs(t)))
    g = -jnp.exp(alog_ref[...]) * softplus
    lane = lax.broadcasted_iota(jnp.int32, (1, LANES), 1)
    gb = jnp.where(lane < GDN_HEADS, g, jax.nn.sigmoid(ab))
    gb_ref[0] = jnp.where(valid, gb, 0.0)


def _inproj_call(h, wc, wqkv, wz, wab, cw, gcw, alog, dtb, tb):
    b, lp, d = h.shape
    full = lambda a: pl.BlockSpec(a.shape, lambda bi, j: (0,) * a.ndim)
    blk = lambda w: pl.BlockSpec((1, tb, w), lambda bi, j: (bi, j, 0))
    out = lambda w, dt: jax.ShapeDtypeStruct((b, lp, w), dt)
    return pl.pallas_call(
        _inproj_kernel,
        grid=(b, lp // tb),
        in_specs=[blk(d), full(wc), full(wqkv), full(wz), full(wab), full(cw), full(gcw), full(alog), full(dtb)],
        out_specs=[blk(CONV_WIDTH), blk(GDN_WIDTH), blk(GDN_WIDTH), blk(GDN_WIDTH), blk(GDN_WIDTH), blk(LANES)],
        out_shape=[out(CONV_WIDTH, BF16), out(GDN_WIDTH, F32), out(GDN_WIDTH, F32), out(GDN_WIDTH, F32),
                   out(GDN_WIDTH, BF16), out(LANES, F32)],
        scratch_shapes=[pltpu.VMEM((SUBLANES, CONV_WIDTH), F32), pltpu.VMEM((SUBLANES, QKV_COLS), F32)],
        compiler_params=_params("parallel", "arbitrary"),
    )(h, wc, wqkv, wz, wab, cw, gcw, alog, dtb)


GDN_BASE = 16


def _gdn_local_kernel(q_ref, k_ref, v_ref, gb_ref, w_ref, qd_ref, kd_ref, u_ref, attn_ref, egl_ref, *, chunks):
    c = GDN_CHUNK
    hi = lax.Precision.HIGHEST
    rows_all = chunks * c

    ri = lax.broadcasted_iota(jnp.int32, (c, c), 0)
    ci = lax.broadcasted_iota(jnp.int32, (c, c), 1)
    causal = ri >= ci
    strict = ri > ci
    eye = (ri == ci).astype(F32)
    base = jnp.logical_and(strict, (ri // GDN_BASE) == (ci // GDN_BASE))
    levels = [jnp.logical_and(jnp.logical_and((ri >> (l + 1)) == (ci >> (l + 1)), ((ri >> l) & 1) == 1),
                              ((ci >> l) & 1) == 0) for l in range(GDN_BASE.bit_length() - 1, c.bit_length() - 1)]
    sel = (lax.broadcasted_iota(jnp.int32, (SUBLANES, LANES), 0)
           == lax.broadcasted_iota(jnp.int32, (SUBLANES, LANES), 1)).astype(F32)

    gb_all = gb_ref[...]
    pos = lax.broadcasted_iota(jnp.int32, (rows_all, 1), 0) % c
    gcum_all = gb_all
    for sh in (1 << k for k in range(c.bit_length() - 1)):
        gcum_all = gcum_all + jnp.where(pos >= sh, pltpu.roll(gcum_all, sh, 0), 0.0)
    gcum_t_all = _mm_nt(sel, gcum_all, hi)

    for cc in range(chunks):
        last = cc * c + c - 1
        egl_ref[cc] = jnp.exp(jnp.broadcast_to(gcum_t_all[:, last:last + 1], (SUBLANES, LANES)))

    items = [(cc, hh) for cc in range(chunks) for hh in range(GDN_HEADS)]
    each = lambda fn, *lists: [fn(*args) for args in zip(*lists)]
    rows_of = [slice(cc * c, (cc + 1) * c) for cc, _ in items]
    cols_of = [slice(hh * GDN_DK, (hh + 1) * GDN_DK) for _, hh in items]

    kh = each(lambda r, l: k_ref[r, l], rows_of, cols_of)
    khb = each(lambda x: x.astype(BF16), kh)
    beta = [gb_all[r, GDN_HEADS + hh:GDN_HEADS + hh + 1] for r, (_, hh) in zip(rows_of, items)]
    gcol = [gcum_all[r, hh:hh + 1] for r, (_, hh) in zip(rows_of, items)]
    grow = [gcum_t_all[hh:hh + 1, r] for r, (_, hh) in zip(rows_of, items)]
    decay = each(lambda gc_, gr_: jnp.where(causal, jnp.exp(jnp.where(causal, gc_ - gr_, 0.0)), 0.0), gcol, grow)
    kbeta = each(lambda x, b_: x * b_, kh, beta)
    a = each(lambda kb_, k_, d_: jnp.where(strict, _mm_nt(kb_.astype(BF16), k_) * d_, 0.0), kbeta, khb, decay)
    n1 = each(lambda a_: jnp.where(base, -a_, 0.0), a)
    n1b = each(lambda x: x.astype(BF16), n1)
    n2 = each(lambda x: _mm(x, x), n1b)
    npow = each(lambda x: x.astype(BF16), n2)
    tinv = each(lambda n1_, n2_, n1b_, n2b_: eye + n1_ + n2_ + _mm(n1b_, n2b_), n1, n2, n1b, npow)
    for _ in range(GDN_BASE.bit_length() - 3):
        npow = each(lambda x: _mm(x, x).astype(BF16), npow)
        tinv = each(lambda t_, p_: t_ + _mm(t_.astype(BF16), p_), tinv, npow)
    for m in levels:
        tb16 = each(lambda t_: t_.astype(BF16), tinv)
        mt = each(lambda a_, t_: _mm(jnp.where(m, a_, 0.0).astype(BF16), t_).astype(BF16), a, tb16)
        tinv = each(lambda t_, tb_, mt_: t_ - _mm(tb_, mt_), tinv, tb16, mt)
    tb16 = each(lambda t_: t_.astype(BF16), tinv)
    egc = each(jnp.exp, gcol)
    vh = each(lambda r, l: v_ref[r, l], rows_of, cols_of)
    rhs = each(lambda v_, b_, kb_, e_: jnp.concatenate([v_ * b_, kb_ * e_], axis=1).astype(BF16),
               vh, beta, kbeta, egc)
    uw = each(_mm, tb16, rhs)
    qh = each(lambda r, l: q_ref[r, l] * (GDN_DK ** -0.5), rows_of, cols_of)
    attn = each(lambda q_, k_, d_: jnp.where(causal, _mm_nt(q_.astype(BF16), k_) * d_, 0.0).astype(BF16),
                qh, khb, decay)
    for idx, (cc, hh) in enumerate(items):
        r, l = rows_of[idx], cols_of[idx]
        u_ref[r, l] = uw[idx][:, :GDN_DV]
        w_ref[r, l] = uw[idx][:, GDN_DV:].astype(BF16)
        attn_ref[r, hh * c:(hh + 1) * c] = attn[idx]
        glast = gcol[idx][c - 1:c, :]
        kd_ref[r, l] = (kh[idx] * jnp.exp(glast - gcol[idx])).astype(BF16)
        qd_ref[r, l] = (qh[idx] * egc[idx]).astype(BF16)


def _gdn_local_call(q, k, v, gb, chunks):
    n, w = q.shape
    tb = chunks * GDN_CHUNK
    blk = lambda width: pl.BlockSpec((tb, width), lambda i: (i, 0))
    out = lambda width, dt: jax.ShapeDtypeStruct((n, width), dt)
    return pl.pallas_call(
        functools.partial(_gdn_local_kernel, chunks=chunks),
        grid=(n // tb,),
        in_specs=[blk(w), blk(w), blk(w), blk(LANES)],
        out_specs=[blk(w), blk(w), blk(w), blk(w), blk(GDN_HEADS * GDN_CHUNK),
                   pl.BlockSpec((chunks, SUBLANES, LANES), lambda i: (i, 0, 0))],
        out_shape=[out(w, BF16), out(w, BF16), out(w, BF16), out(w, F32), out(GDN_HEADS * GDN_CHUNK, BF16),
                   jax.ShapeDtypeStruct((n // GDN_CHUNK, SUBLANES, LANES), F32)],
        compiler_params=_params("parallel"),
    )(q, k, v, gb)


def _gdn_scan_kernel(w_ref, qd_ref, kd_ref, u_ref, attn_ref, egl_ref, o_ref, state, *, chunks):
    c = GDN_CHUNK
    bsz = w_ref.shape[0]

    @pl.when(pl.program_id(0) == 0)
    def _():
        state[...] = jnp.zeros_like(state)

    items = [(b, hh) for b in range(bsz) for hh in range(GDN_HEADS)]
    each = lambda fn, *lists: [fn(*args) for args in zip(*lists)]
    cols_of = [slice(hh * GDN_DK, (hh + 1) * GDN_DK) for _, hh in items]
    sh = [state[idx] for idx in range(len(items))]
    for cc in range(chunks):
        rows = slice(cc * c, (cc + 1) * c)
        sb = each(lambda s_: s_.astype(BF16), sh)
        wq = [jnp.concatenate([w_ref[b, rows, l], qd_ref[b, rows, l]], axis=0) for (b, _), l in zip(items, cols_of)]
        r = each(_mm, wq, sb)
        vnb = [(u_ref[b, rows, l] - r_[:c]).astype(BF16) for (b, _), l, r_ in zip(items, cols_of, r)]
        for (b, hh), l, r_, v_ in zip(items, cols_of, r, vnb):
            o_ref[b, rows, l] = r_[c:] + _mm(attn_ref[b, rows, hh * c:(hh + 1) * c], v_)
        sh = [s_ * egl_ref[b, cc, hh:hh + 1, :] + _mm_tn(kd_ref[b, rows, l], v_)
              for (b, hh), l, s_, v_ in zip(items, cols_of, sh, vnb)]
    for idx, s_ in enumerate(sh):
        state[idx] = s_


def _gdn_scan_call(w, qd, kd, u, attn, egl, chunks):
    b, lp, width = u.shape
    tb = chunks * GDN_CHUNK
    blk = lambda wd: pl.BlockSpec((b, tb, wd), lambda j: (0, j, 0))
    return pl.pallas_call(
        functools.partial(_gdn_scan_kernel, chunks=chunks),
        grid=(lp // tb,),
        in_specs=[blk(width), blk(width), blk(width), blk(width), blk(GDN_HEADS * GDN_CHUNK),
                  pl.BlockSpec((b, chunks, SUBLANES, LANES), lambda j: (0, j, 0, 0))],
        out_specs=blk(width),
        out_shape=jax.ShapeDtypeStruct((b, lp, width), F32),
        scratch_shapes=[pltpu.VMEM((b * GDN_HEADS, GDN_DK, GDN_DV), F32)],
        compiler_params=_params("arbitrary"),
    )(w, qd, kd, u, attn, egl)


def _outproj_kernel(o_ref, sz_ref, yconv_ref, h_ref, wtop_ref, wbot_ref, nw_ref, g_ref, b_ref, out_ref, *, alpha):
    o = o_ref[...]
    parts = []
    for hh in range(GDN_HEADS):
        oh = o[:, hh * GDN_DV:(hh + 1) * GDN_DV]
        parts.append(oh * lax.rsqrt(jnp.mean(oh * oh, axis=-1, keepdims=True) + RMS_EPS) * nw_ref[...])
    ygdn = (jnp.concatenate(parts, axis=-1) * sz_ref[...].astype(F32)).astype(BF16)
    mix = _mm(yconv_ref[...], wtop_ref[...]) + _mm(ygdn, wbot_ref[...])
    out_ref[...] = _layer_norm(alpha * h_ref[...] + mix, g_ref[...], b_ref[...])


def _outproj_call(o, sz, yconv, h, wtop, wbot, nw, g, b, tb, alpha):
    n, d = h.shape
    full = lambda a: pl.BlockSpec(a.shape, lambda i: (0,) * a.ndim)
    blk = lambda w: pl.BlockSpec((tb, w), lambda i: (i, 0))
    return pl.pallas_call(
        functools.partial(_outproj_kernel, alpha=alpha),
        grid=(n // tb,),
        in_specs=[blk(GDN_WIDTH), blk(GDN_WIDTH), blk(CONV_WIDTH), blk(d), full(wtop), full(wbot), full(nw),
                  full(g), full(b)],
        out_specs=blk(d),
        out_shape=jax.ShapeDtypeStruct((n, d), F32),
        compiler_params=_params("parallel"),
    )(o, sz, yconv, h, wtop, wbot, nw, g, b)


def _sorting_network(n):
    pairs = []
    p = 1
    while p < n:
        k = p
        while k >= 1:
            for j in range(k % p, n - k, 2 * k):
                for i in range(min(k, n - j - k)):
                    if (i + j) // (2 * p) == (i + j + k) // (2 * p):
                        pairs.append((i + j, i + j + k))
            k //= 2
        p *= 2
    return pairs


def _top_values_sorted(s, n):
    groups = s.shape[0] // SUBLANES
    cols = [s[g * SUBLANES:(g + 1) * SUBLANES] for g in range(groups)]
    for a, b in _sorting_network(groups):
        cols[a], cols[b] = jnp.maximum(cols[a], cols[b]), jnp.minimum(cols[a], cols[b])
    tops = []
    for r in range(n):
        m = jnp.max(cols[0], axis=0, keepdims=True)
        tops.append(m)
        if r + 1 < n:
            hit = cols[0] >= m
            for k in range(min(groups, n - 1 - r)):
                below = cols[k + 1] if k + 1 < groups else NEG_INF
                cols[k] = jnp.where(hit, below, cols[k])
    return jnp.concatenate(tops, axis=0)


def _peer_kernel(xa_ref, xc_ref, wq_ref, k1_ref, k2_ref, u_ref, vt_ref, g_ref, b_ref, out_ref,
                 xb_ref, acc_ref, ht_ref, coef_ref, th_ref, aa_ref, s2_ref, bb_ref, *, alpha, ec, ne, total):
    s = pl.program_id(0)
    tb = xa_ref.shape[0]
    nk = PEER_TOPK + 1
    chunk_a = jnp.clip(s, 0, total - 1) % ne
    chunk_b = (s - 1) % ne
    chunk_c = (s - 2) % ne

    @pl.when(s == 0)
    def _():
        ht_ref[...] = jnp.zeros_like(ht_ref)
        coef_ref[...] = jnp.zeros_like(coef_ref)
        th_ref[...] = jnp.zeros_like(th_ref)
        aa_ref[...] = jnp.zeros_like(aa_ref)
        s2_ref[...] = jnp.zeros_like(s2_ref)
        bb_ref[...] = jnp.zeros_like(bb_ref)
        acc_ref[...] = jnp.zeros_like(acc_ref)

    @pl.when(jnp.logical_and(s < total, chunk_a == 0))
    def _():
        xb_ref[...] = xa_ref[...].astype(BF16)

    @pl.when(jnp.logical_and(jnp.logical_and(s >= 1, s <= total), chunk_b == 0))
    def _():
        xb = xb_ref[...]
        for hd in range(PEER_HEADS):
            lo = hd * 2 * PEER_HALF
            q = _mm(xb, wq_ref[:, lo:lo + 2 * PEER_HALF]).astype(BF16)
            s1_all = _mm_nt(k1_ref[...], q[:, :PEER_HALF])
            s2_all = _mm_nt(k2_ref[...], q[:, PEER_HALF:])
            for lt in range(tb // LANES):
                ls = slice(lt * LANES, (lt + 1) * LANES)
                s1 = s1_all[:, ls]
                s2 = s2_all[:, ls]
                t1 = _top_values_sorted(s1, nk)
                t2 = _top_values_sorted(s2, nk)
                cand = [t1[r:r + 1] + t2[:nk // (r + 1)] for r in range(nk)]
                fill = -sum(c.shape[0] for c in cand) % (SUBLANES * SUBLANES)
                cand = jnp.concatenate(cand + [jnp.full((fill, LANES), NEG_INF, F32)], axis=0)
                best = _top_values_sorted(cand, nk)
                tau = 0.5 * (best[PEER_TOPK - 1:PEER_TOPK] + best[PEER_TOPK:PEER_TOPK + 1])
                z = jnp.sum(jnp.exp(best[:PEER_TOPK] - best[0:1]), axis=0, keepdims=True)
                th = tau - s1
                aa = 0.5 * jnp.exp(s1 - t1[0:1]) / z
                for grp in range(PEER_NKEYS // SUBLANES):
                    th_ref[hd, grp, :, ls] = th[grp * SUBLANES:(grp + 1) * SUBLANES]
                    aa_ref[hd, grp, :, ls] = aa[grp * SUBLANES:(grp + 1) * SUBLANES]
                s2_ref[hd, :, ls] = s2
                bb_ref[hd, :, ls] = jnp.exp(s2 - t2[0:1])

    @pl.when(jnp.logical_and(s >= 2, chunk_c == 0))
    def _():
        acc_ref[...] = jnp.zeros_like(acc_ref)

    rows_per = PEER_NKEYS
    assert ec == SUBLANES * rows_per

    def stage_a(ht_w, ts, ms):
        ht_w[ms, ts] = _mm_nt(u_ref[ms, :], xb_ref[ts, :])

    def stage_b(ht_r, coef_w, ig, jh, lt):
        ls = slice(lt * LANES, (lt + 1) * LANES)
        js = slice(jh * (rows_per // PEER_JSPLIT), (jh + 1) * (rows_per // PEER_JSPLIT))
        iis = range(ig * PEER_IGROUP, (ig + 1) * PEER_IGROUP)
        gsum = [jnp.zeros((rows_per // PEER_JSPLIT, LANES), F32) for _ in iis]
        for hd in range(PEER_HEADS):
            s2 = s2_ref[hd, js, ls]
            bb = bb_ref[hd, js, ls]
            for k, ii in enumerate(iis):
                th = th_ref[hd, chunk_b, ii:ii + 1, ls]
                aa = aa_ref[hd, chunk_b, ii:ii + 1, ls]
                gsum[k] = gsum[k] + jnp.where(s2 >= th, bb, 0.0) * aa
        for k, ii in enumerate(iis):
            rs = slice(ii * rows_per + js.start, ii * rows_per + js.stop)
            ht = ht_r[rs, ls]
            act2 = ht * (1.0 + lax.erf(ht * np.float32(np.sqrt(0.5))))
            coef_w[rs, ls] = (gsum[k] * act2).astype(BF16)

    def stage_c(coef_r, ts, ms):
        acc_ref[ms, ts] += _mm(vt_ref[0, ms, :], coef_r[:, ts])

    assert acc_ref.shape[0] == ec
    mxu_pieces = [(slice(t0, t0 + PEER_SLICE), slice(m0, m0 + PEER_MROWS))
                  for t0 in range(0, tb, PEER_SLICE) for m0 in range(0, ec, PEER_MROWS)]
    valu_pieces = [(ig, jh, lt) for lt in range(tb // LANES) for ig in range(SUBLANES // PEER_IGROUP)
                   for jh in range(PEER_JSPLIT)]
    per = -(-len(valu_pieces) // len(mxu_pieces))

    def stages(ht_w, ht_r, coef_w, coef_r):
        for idx, (ts, ms) in enumerate(mxu_pieces):
            stage_a(ht_w, ts, ms)
            stage_c(coef_r, ts, ms)
            for piece in valu_pieces[idx * per:(idx + 1) * per]:
                stage_b(ht_r, coef_w, *piece)

    slot_a = s % 2
    slot_b = (s + 1) % 2
    stages(ht_ref.at[slot_a], ht_ref.at[slot_b], coef_ref.at[slot_b], coef_ref.at[slot_a])

    @pl.when(jnp.logical_and(s >= 2, chunk_c == ne - 1))
    def _():
        y = acc_ref[...].T
        out_ref[...] = _layer_norm(alpha * xc_ref[...] + y, g_ref[...], b_ref[...])


PEER_SLICE = 256
PEER_MROWS = 256
PEER_IGROUP = 4
PEER_JSPLIT = 2


def _peer_call(x, wq, k1, k2, u, vt, g, b, tb, ec, alpha):
    n, d = x.shape
    ne = u.shape[0] // ec
    total = (n // tb) * ne
    once = pl.Buffered(1)
    full = lambda a: pl.BlockSpec(a.shape, lambda s: (0,) * a.ndim, pipeline_mode=once)
    blk_a = lambda s: jnp.clip(s, 0, total - 1) // ne
    blk_c = lambda s: jnp.clip(s - 2, 0, total - 1) // ne
    head_scratch = pltpu.VMEM((PEER_HEADS, PEER_NKEYS, tb), F32)
    grouped_scratch = pltpu.VMEM((PEER_HEADS, PEER_NKEYS // SUBLANES, SUBLANES, tb), F32)
    return pl.pallas_call(
        functools.partial(_peer_kernel, alpha=alpha, ec=ec, ne=ne, total=total),
        grid=(total + 2,),
        in_specs=[pl.BlockSpec((tb, d), lambda s: (blk_a(s), 0), pipeline_mode=once),
                  pl.BlockSpec((tb, d), lambda s: (blk_c(s), 0), pipeline_mode=once),
                  full(wq), full(k1), full(k2),
                  pl.BlockSpec((ec, d), lambda s: (jnp.clip(s, 0, total - 1) % ne, 0)),
                  pl.BlockSpec((1, d, ec), lambda s: (jnp.clip(s - 2, 0, total - 1) % ne, 0, 0)),
                  full(g), full(b)],
        out_specs=pl.BlockSpec((tb, d), lambda s: (blk_c(s), 0)),
        out_shape=jax.ShapeDtypeStruct((n, d), F32),
        scratch_shapes=[pltpu.VMEM((tb, d), BF16), pltpu.VMEM((d, tb), F32),
                        pltpu.VMEM((2, ec, tb), F32), pltpu.VMEM((2, ec, tb), BF16),
                        grouped_scratch, grouped_scratch, head_scratch, head_scratch],
        compiler_params=_params("arbitrary"),
    )(x, x, wq, k1, k2, u, vt, g, b)


def kernel(x, meta_tokens, ln_in_g, ln_in_b, w_in, conv_w, gdn_conv_w, a_log, dt_bias, gdn_norm_w, w_out, ln1_g,
           ln1_b, peer_w_q, peer_k1, peer_k2, peer_u, peer_v, ln2_g, ln2_b):
    bsz, seq, d = x.shape
    depth = w_in.shape[0]
    alpha = (2.0 * depth) ** 0.25
    lp = SEQ_PAD + N_META + seq
    n = bsz * lp
    assert lp % GDN_CHUNK == 0

    tb_row = _pick_block(n, 768, LANES)
    tb_in = _pick_block(lp, 768, 2 * SUBLANES)
    chunks = 3 if (lp // GDN_CHUNK) % 3 == 0 else 1
    chunks_local = 12 if (n // GDN_CHUNK) % 12 == 0 else chunks
    tb_peer = _pick_block(n, 768, PEER_SLICE)
    ec = SUBLANES * PEER_NKEYS
    n_exp = peer_u.shape[1]

    row = lambda a: a.reshape(1, -1).astype(F32)
    pad_lanes = lambda a: jnp.pad(a.reshape(1, -1).astype(F32), ((0, 0), (0, LANES - a.size)))

    meta = jnp.broadcast_to(meta_tokens[None].astype(x.dtype), (bsz, N_META, d))
    h = jnp.concatenate([jnp.zeros((bsz, SEQ_PAD, d), x.dtype), meta, x], axis=1).reshape(n, d)
    h = _ln_call(h, row(ln_in_g), row(ln_in_b), tb_row)

    c3 = 3 * CONV_WIDTH
    for l in range(depth):
        wl = w_in[l]
        wc = wl[:, :c3].astype(BF16)
        wqkv = wl[:, c3:c3 + QKV_COLS].astype(BF16)
        wz = wl[:, c3 + QKV_COLS:c3 + QKV_COLS + GDN_WIDTH].astype(BF16)
        wab = jnp.pad(wl[:, c3 + QKV_COLS + GDN_WIDTH:], ((0, 0), (0, LANES - 2 * GDN_HEADS))).astype(BF16)
        yconv, q, k, v, sz, gb = _inproj_call(
            h.reshape(bsz, lp, d), wc, wqkv, wz, wab, conv_w[l].astype(F32), gdn_conv_w[l].astype(F32),
            pad_lanes(a_log[l]), pad_lanes(dt_bias[l]), tb_in)
        flat = lambda a: a.reshape(n, a.shape[-1])
        wg, qd, kd, ug, attn, egl = _gdn_local_call(flat(q), flat(k), flat(v), flat(gb), chunks_local)
        seq = lambda a: a.reshape(bsz, lp, a.shape[-1])
        o = _gdn_scan_call(seq(wg), seq(qd), seq(kd), seq(ug), seq(attn),
                           egl.reshape(bsz, lp // GDN_CHUNK, SUBLANES, LANES), chunks)
        wo = w_out[l].astype(BF16)
        h = _outproj_call(o.reshape(n, GDN_WIDTH), sz.reshape(n, GDN_WIDTH), yconv.reshape(n, CONV_WIDTH), h,
                          wo[:CONV_WIDTH], wo[CONV_WIDTH:], row(gdn_norm_w[l]), row(ln1_g[l]), row(ln1_b[l]),
                          tb_row, alpha)
        ub = peer_u[l].astype(BF16)
        vt = peer_v[l].astype(BF16).reshape(n_exp // ec, ec, d).transpose(0, 2, 1)
        h = _peer_call(h, peer_w_q[l].astype(BF16), peer_k1[l].astype(BF16), peer_k2[l].astype(BF16), ub, vt,
                       row(ln2_g[l]), row(ln2_b[l]), tb_peer, ec, alpha)
    return h.reshape(bsz, lp, d)[:, SEQ_PAD + N_META:]
```

```python
import functools

import jax
import jax.numpy as jnp
import numpy as np
from jax import lax
from jax.experimental import pallas as pl
from jax.experimental.pallas import tpu as pltpu

F32 = jnp.float32
BF16 = jnp.bfloat16

N_META = 16
CONV_WIDTH = 512
CONV_K = 3
GDN_HEADS = 4
GDN_DK = 128
GDN_DV = 128
GDN_CONV_K = 4
GDN_CHUNK = 64
QKV_COLS = GDN_HEADS * (2 * GDN_DK + GDN_DV)
GDN_WIDTH = GDN_HEADS * GDN_DV
PEER_HEADS = 8
PEER_NKEYS = 128
PEER_HALF = 128
PEER_TOPK = 16
LN_EPS = 1e-5
RMS_EPS = 1e-6
SEQ_PAD = (-N_META) % GDN_CHUNK
LANES = 128
SUBLANES = 8
VMEM_LIMIT = 56 * 1024 * 1024
NEG_INF = float("-inf")


def _mm(a, b, precision=None):
    return jnp.dot(a, b, preferred_element_type=F32, precision=precision)


def _mm_nt(a, b, precision=None):
    return lax.dot_general(a, b, (((1,), (1,)), ((), ())), preferred_element_type=F32, precision=precision)


def _mm_tn(a, b, precision=None):
    return lax.dot_general(a, b, (((0,), (0,)), ((), ())), preferred_element_type=F32, precision=precision)


def _layer_norm(x, g, b):
    mu = jnp.mean(x, axis=-1, keepdims=True)
    xc = x - mu
    var = jnp.mean(xc * xc, axis=-1, keepdims=True)
    return xc * lax.rsqrt(var + LN_EPS) * g + b


def _silu(x):
    return x * jax.nn.sigmoid(x)


def _pick_block(n, target, mult):
    best = None
    for d in range(mult, min(n, target) + 1, mult):
        if n % d == 0:
            best = d
    assert best is not None, (n, target, mult)
    return best


def _params(*sem):
    return pltpu.CompilerParams(dimension_semantics=sem, vmem_limit_bytes=VMEM_LIMIT)


LN_IN_ROWS = 3 * GDN_CHUNK


def _ln_in_kernel(xprev_ref, xcur_ref, meta_ref, g_ref, b_ref, o_ref):
    d = o_ref.shape[-1]
    lead = jnp.concatenate([jnp.zeros((SEQ_PAD, d), F32), meta_ref[...]], axis=0)
    head = jnp.where(pl.program_id(1) == 0, lead, xprev_ref[0, LN_IN_ROWS - GDN_CHUNK:, :])
    rows = jnp.concatenate([head, xcur_ref[0, :LN_IN_ROWS - GDN_CHUNK, :]], axis=0)
    o_ref[0] = _layer_norm(rows, g_ref[...], b_ref[...])


def _ln_in_call(x, meta, g, b, lp):
    bsz, seq, d = x.shape
    assert SEQ_PAD + N_META == GDN_CHUNK and lp % LN_IN_ROWS == 0
    last = pl.cdiv(seq, LN_IN_ROWS) - 1
    full = lambda a: pl.BlockSpec(a.shape, lambda bi, j: (0,) * a.ndim)
    return pl.pallas_call(
        _ln_in_kernel,
        grid=(bsz, lp // LN_IN_ROWS),
        in_specs=[pl.BlockSpec((1, LN_IN_ROWS, d), lambda bi, j: (bi, jnp.maximum(j - 1, 0), 0)),
                  pl.BlockSpec((1, LN_IN_ROWS, d), lambda bi, j: (bi, jnp.minimum(j, last), 0)),
                  full(meta), full(g), full(b)],
        out_specs=pl.BlockSpec((1, LN_IN_ROWS, d), lambda bi, j: (bi, j, 0)),
        out_shape=jax.ShapeDtypeStruct((bsz, lp, d), F32),
        compiler_params=_params("parallel", "parallel"),
    )(x, x, meta, g, b)


def _causal_conv(u, prev, w):
    k = w.shape[0]

    def taps(a):
        acc = a * w[k - 1:k]
        for s in range(1, k):
            acc = acc + pltpu.roll(a, s, 0) * w[k - 1 - s:k - s]
        return acc

    body = taps(u)
    head = taps(jnp.concatenate([prev, u[:SUBLANES]], axis=0))[SUBLANES:]
    return jnp.concatenate([head, body[SUBLANES:]], axis=0)


def _inproj_kernel(h_ref, wc_ref, wqkv_ref, wz_ref, wab_ref, cw_ref, gcw_ref, alog_ref, dtb_ref,
                   yconv_ref, q_ref, k_ref, v_ref, sz_ref, gb_ref, carry_c, carry_qkv):
    j = pl.program_id(1)
    tb = h_ref.shape[1]

    @pl.when(j == 0)
    def _():
        carry_c[...] = jnp.zeros_like(carry_c)
        carry_qkv[...] = jnp.zeros_like(carry_qkv)

    xb = h_ref[0].astype(BF16)
    row = j * tb + lax.broadcasted_iota(jnp.int32, (tb, 1), 0)
    valid = row >= SEQ_PAD

    p = _mm(xb, wc_ref[...])
    u = jnp.where(valid, p[:, CONV_WIDTH:2 * CONV_WIDTH] * p[:, 2 * CONV_WIDTH:], 0.0)
    yconv_ref[0] = (p[:, :CONV_WIDTH] * _causal_conv(u, carry_c[...], cw_ref[...])).astype(BF16)
    carry_c[...] = u[tb - SUBLANES:]

    p = jnp.where(valid, _mm(xb, wqkv_ref[...]), 0.0)
    s = _silu(_causal_conv(p, carry_qkv[...], gcw_ref[...]))
    carry_qkv[...] = p[tb - SUBLANES:]
    for hh in range(GDN_HEADS):
        lo = hh * GDN_DK
        qh = s[:, lo:lo + GDN_DK]
        kh = s[:, GDN_WIDTH + lo:GDN_WIDTH + lo + GDN_DK]
        q_ref[0, :, lo:lo + GDN_DK] = qh * lax.rsqrt(jnp.sum(qh * qh, axis=-1, keepdims=True) + RMS_EPS)
        k_ref[0, :, lo:lo + GDN_DK] = kh * lax.rsqrt(jnp.sum(kh * kh, axis=-1, keepdims=True) + RMS_EPS)
    v_ref[0] = s[:, 2 * GDN_WIDTH:]

    sz_ref[0] = _silu(_mm(xb, wz_ref[...])).astype(BF16)

    ab = _mm(xb, wab_ref[...])
    t = ab + dtb_ref[...]
    softplus = jnp.maximum(t, 0.0) + jnp.log1p(jnp.exp(-jnp.abs(t)))
    g = -jnp.exp(alog_ref[...]) * softplus
    lane = lax.broadcasted_iota(jnp.int32, (1, LANES), 1)
    gb = jnp.where(lane < GDN_HEADS, g, jax.nn.sigmoid(ab))
    gb_ref[0] = jnp.where(valid, gb, 0.0)


def _inproj_call(h, wc, wqkv, wz, wab, cw, gcw, alog, dtb, tb):
    b, lp, d = h.shape
    full = lambda a: pl.BlockSpec(a.shape, lambda bi, j: (0,) * a.ndim)
    blk = lambda w: pl.BlockSpec((1, tb, w), lambda bi, j: (bi, j, 0))
    out = lambda w, dt: jax.ShapeDtypeStruct((b, lp, w), dt)
    return pl.pallas_call(
        _inproj_kernel,
        grid=(b, lp // tb),
        in_specs=[blk(d), full(wc), full(wqkv), full(wz), full(wab), full(cw), full(gcw), full(alog), full(dtb)],
        out_specs=[blk(CONV_WIDTH), blk(GDN_WIDTH), blk(GDN_WIDTH), blk(GDN_WIDTH), blk(GDN_WIDTH), blk(LANES)],
        out_shape=[out(CONV_WIDTH, BF16), out(GDN_WIDTH, F32), out(GDN_WIDTH, F32), out(GDN_WIDTH, F32),
                   out(GDN_WIDTH, BF16), out(LANES, F32)],
        scratch_shapes=[pltpu.VMEM((SUBLANES, CONV_WIDTH), F32), pltpu.VMEM((SUBLANES, QKV_COLS), F32)],
        compiler_params=_params("parallel", "arbitrary"),
    )(h, wc, wqkv, wz, wab, cw, gcw, alog, dtb)


GDN_BASE = 16


def _gdn_local_kernel(q_ref, k_ref, v_ref, gb_ref, w_ref, qd_ref, kd_ref, u_ref, attn_ref, egl_ref, *, chunks):
    c = GDN_CHUNK
    hi = lax.Precision.HIGHEST
    rows_all = chunks * c

    ri = lax.broadcasted_iota(jnp.int32, (c, c), 0)
    ci = lax.broadcasted_iota(jnp.int32, (c, c), 1)
    causal = ri >= ci
    strict = ri > ci
    eye = (ri == ci).astype(F32)
    base = jnp.logical_and(strict, (ri // GDN_BASE) == (ci // GDN_BASE))
    levels = [jnp.logical_and(jnp.logical_and((ri >> (l + 1)) == (ci >> (l + 1)), ((ri >> l) & 1) == 1),
                              ((ci >> l) & 1) == 0) for l in range(GDN_BASE.bit_length() - 1, c.bit_length() - 1)]
    sel = (lax.broadcasted_iota(jnp.int32, (SUBLANES, LANES), 0)
           == lax.broadcasted_iota(jnp.int32, (SUBLANES, LANES), 1)).astype(F32)

    gb_all = gb_ref[...]
    pos = lax.broadcasted_iota(jnp.int32, (rows_all, 1), 0) % c
    gcum_all = gb_all
    for sh in (1 << k for k in range(c.bit_length() - 1)):
        gcum_all = gcum_all + jnp.where(pos >= sh, pltpu.roll(gcum_all, sh, 0), 0.0)
    gcum_t_all = _mm_nt(sel, gcum_all, hi)

    for cc in range(chunks):
        last = cc * c + c - 1
        egl_ref[cc] = jnp.exp(jnp.broadcast_to(gcum_t_all[:, last:last + 1], (SUBLANES, LANES)))

    items = [(cc, hh) for cc in range(chunks) for hh in range(GDN_HEADS)]
    each = lambda fn, *lists: [fn(*args) for args in zip(*lists)]
    rows_of = [slice(cc * c, (cc + 1) * c) for cc, _ in items]
    cols_of = [slice(hh * GDN_DK, (hh + 1) * GDN_DK) for _, hh in items]

    kh = each(lambda r, l: k_ref[r, l], rows_of, cols_of)
    khb = each(lambda x: x.astype(BF16), kh)
    beta = [gb_all[r, GDN_HEADS + hh:GDN_HEADS + hh + 1] for r, (_, hh) in zip(rows_of, items)]
    gcol = [gcum_all[r, hh:hh + 1] for r, (_, hh) in zip(rows_of, items)]
    grow = [gcum_t_all[hh:hh + 1, r] for r, (_, hh) in zip(rows_of, items)]
    decay = each(lambda gc_, gr_: jnp.where(causal, jnp.exp(jnp.where(causal, gc_ - gr_, 0.0)), 0.0), gcol, grow)
    kbeta = each(lambda x, b_: x * b_, kh, beta)
    a = each(lambda kb_, k_, d_: jnp.where(strict, _mm_nt(kb_.astype(BF16), k_) * d_, 0.0), kbeta, khb, decay)
    n1 = each(lambda a_: jnp.where(base, -a_, 0.0), a)
    n1b = each(lambda x: x.astype(BF16), n1)
    n2 = each(lambda x: _mm(x, x), n1b)
    npow = each(lambda x: x.astype(BF16), n2)
    tinv = each(lambda n1_, n2_, n1b_, n2b_: eye + n1_ + n2_ + _mm(n1b_, n2b_), n1, n2, n1b, npow)
    for _ in range(GDN_BASE.bit_length() - 3):
        npow = each(lambda x: _mm(x, x).astype(BF16), npow)
        tinv = each(lambda t_, p_: t_ + _mm(t_.astype(BF16), p_), tinv, npow)
    for m in levels:
        tb16 = each(lambda t_: t_.astype(BF16), tinv)
        mt = each(lambda a_, t_: _mm(jnp.where(m, a_, 0.0).astype(BF16), t_).astype(BF16), a, tb16)
        tinv = each(lambda t_, tb_, mt_: t_ - _mm(tb_, mt_), tinv, tb16, mt)
    tb16 = each(lambda t_: t_.astype(BF16), tinv)
    egc = each(jnp.exp, gcol)
    vh = each(lambda r, l: v_ref[r, l], rows_of, cols_of)
    rhs = each(lambda v_, b_, kb_, e_: jnp.concatenate([v_ * b_, kb_ * e_], axis=1).astype(BF16),
               vh, beta, kbeta, egc)
    uw = each(_mm, tb16, rhs)
    qh = each(lambda r, l: q_ref[r, l] * (GDN_DK ** -0.5), rows_of, cols_of)
    attn = each(lambda q_, k_, d_: jnp.where(causal, _mm_nt(q_.astype(BF16), k_) * d_, 0.0).astype(BF16),
                qh, khb, decay)
    for idx, (cc, hh) in enumerate(items):
        r, l = rows_of[idx], cols_of[idx]
        u_ref[r, l] = uw[idx][:, :GDN_DV]
        w_ref[r, l] = uw[idx][:, GDN_DV:].astype(BF16)
        attn_ref[r, hh * c:(hh + 1) * c] = attn[idx]
        glast = gcol[idx][c - 1:c, :]
        kd_ref[r, l] = (kh[idx] * jnp.exp(glast - gcol[idx])).astype(BF16)
        qd_ref[r, l] = (qh[idx] * egc[idx]).astype(BF16)


def _gdn_local_call(q, k, v, gb, chunks):
    n, w = q.shape
    tb = chunks * GDN_CHUNK
    blk = lambda width: pl.BlockSpec((tb, width), lambda i: (i, 0))
    out = lambda width, dt: jax.ShapeDtypeStruct((n, width), dt)
    return pl.pallas_call(
        functools.partial(_gdn_local_kernel, chunks=chunks),
        grid=(n // tb,),
        in_specs=[blk(w), blk(w), blk(w), blk(LANES)],
        out_specs=[blk(w), blk(w), blk(w), blk(w), blk(GDN_HEADS * GDN_CHUNK),
                   pl.BlockSpec((chunks, SUBLANES, LANES), lambda i: (i, 0, 0))],
        out_shape=[out(w, BF16), out(w, BF16), out(w, BF16), out(w, F32), out(GDN_HEADS * GDN_CHUNK, BF16),
                   jax.ShapeDtypeStruct((n // GDN_CHUNK, SUBLANES, LANES), F32)],
        compiler_params=_params("parallel"),
    )(q, k, v, gb)


def _gdn_scan_kernel(w_ref, qd_ref, kd_ref, u_ref, attn_ref, egl_ref, o_ref, state, *, chunks):
    c = GDN_CHUNK
    bsz = w_ref.shape[0]

    @pl.when(pl.program_id(0) == 0)
    def _():
        state[...] = jnp.zeros_like(state)

    items = [(b, hh) for b in range(bsz) for hh in range(GDN_HEADS)]
    each = lambda fn, *lists: [fn(*args) for args in zip(*lists)]
    cols_of = [slice(hh * GDN_DK, (hh + 1) * GDN_DK) for _, hh in items]
    sh = [state[idx] for idx in range(len(items))]
    for cc in range(chunks):
        rows = slice(cc * c, (cc + 1) * c)
        sb = each(lambda s_: s_.astype(BF16), sh)
        wq = [jnp.concatenate([w_ref[b, rows, l], qd_ref[b, rows, l]], axis=0) for (b, _), l in zip(items, cols_of)]
        r = each(_mm, wq, sb)
        vnb = [(u_ref[b, rows, l] - r_[:c]).astype(BF16) for (b, _), l, r_ in zip(items, cols_of, r)]
        for (b, hh), l, r_, v_ in zip(items, cols_of, r, vnb):
            o_ref[b, rows, l] = r_[c:] + _mm(attn_ref[b, rows, hh * c:(hh + 1) * c], v_)
        sh = [s_ * egl_ref[b, cc, hh:hh + 1, :] + _mm_tn(kd_ref[b, rows, l], v_)
              for (b, hh), l, s_, v_ in zip(items, cols_of, sh, vnb)]
    for idx, s_ in enumerate(sh):
        state[idx] = s_


def _gdn_scan_call(w, qd, kd, u, attn, egl, chunks):
    b, lp, width = u.shape
    tb = chunks * GDN_CHUNK
    blk = lambda wd: pl.BlockSpec((b, tb, wd), lambda j: (0, j, 0))
    return pl.pallas_call(
        functools.partial(_gdn_scan_kernel, chunks=chunks),
        grid=(lp // tb,),
        in_specs=[blk(width), blk(width), blk(width), blk(width), blk(GDN_HEADS * GDN_CHUNK),
                  pl.BlockSpec((b, chunks, SUBLANES, LANES), lambda j: (0, j, 0, 0))],
        out_specs=blk(width),
        out_shape=jax.ShapeDtypeStruct((b, lp, width), F32),
        scratch_shapes=[pltpu.VMEM((b * GDN_HEADS, GDN_DK, GDN_DV), F32)],
        compiler_params=_params("arbitrary"),
    )(w, qd, kd, u, attn, egl)


def _outproj_kernel(o_ref, sz_ref, yconv_ref, h_ref, wtop_ref, wbot_ref, nw_ref, g_ref, b_ref, out_ref, *, alpha):
    o = o_ref[...]
    parts = []
    for hh in range(GDN_HEADS):
        oh = o[:, hh * GDN_DV:(hh + 1) * GDN_DV]
        parts.append(oh * lax.rsqrt(jnp.mean(oh * oh, axis=-1, keepdims=True) + RMS_EPS) * nw_ref[...])
    ygdn = (jnp.concatenate(parts, axis=-1) * sz_ref[...].astype(F32)).astype(BF16)
    mix = _mm(yconv_ref[...], wtop_ref[...]) + _mm(ygdn, wbot_ref[...])
    out_ref[...] = _layer_norm(alpha * h_ref[...] + mix, g_ref[...], b_ref[...])


def _outproj_call(o, sz, yconv, h, wtop, wbot, nw, g, b, tb, alpha):
    n, d = h.shape
    full = lambda a: pl.BlockSpec(a.shape, lambda i: (0,) * a.ndim)
    blk = lambda w: pl.BlockSpec((tb, w), lambda i: (i, 0))
    return pl.pallas_call(
        functools.partial(_outproj_kernel, alpha=alpha),
        grid=(n // tb,),
        in_specs=[blk(GDN_WIDTH), blk(GDN_WIDTH), blk(CONV_WIDTH), blk(d), full(wtop), full(wbot), full(nw),
                  full(g), full(b)],
        out_specs=blk(d),
        out_shape=jax.ShapeDtypeStruct((n, d), F32),
        compiler_params=_params("parallel"),
    )(o, sz, yconv, h, wtop, wbot, nw, g, b)


def _sorting_network(n):
    pairs = []
    p = 1
    while p < n:
        k = p
        while k >= 1:
            for j in range(k % p, n - k, 2 * k):
                for i in range(min(k, n - j - k)):
                    if (i + j) // (2 * p) == (i + j + k) // (2 * p):
                        pairs.append((i + j, i + j + k))
            k //= 2
        p *= 2
    return pairs


def _top_values_sorted(s, n):
    groups = s.shape[0] // SUBLANES
    cols = [s[g * SUBLANES:(g + 1) * SUBLANES] for g in range(groups)]
    for a, b in _sorting_network(groups):
        cols[a], cols[b] = jnp.maximum(cols[a], cols[b]), jnp.minimum(cols[a], cols[b])
    tops = []
    for r in range(n):
        m = jnp.max(cols[0], axis=0, keepdims=True)
        tops.append(m)
        if r + 1 < n:
            hit = cols[0] >= m
            for k in range(min(groups, n - 1 - r)):
                below = cols[k + 1] if k + 1 < groups else NEG_INF
                cols[k] = jnp.where(hit, below, cols[k])
    return jnp.concatenate(tops, axis=0)


def _peer_kernel(xa_ref, xc_ref, wq_ref, k1_ref, k2_ref, u_ref, vt_ref, g_ref, b_ref, out_ref,
                 xb_ref, acc_ref, ht_ref, coef_ref, th_ref, aa_ref, s2_ref, bb_ref, *, alpha, ec, ne, total):
    s = pl.program_id(0)
    tb = xa_ref.shape[0]
    nk = PEER_TOPK + 1
    chunk_a = jnp.clip(s, 0, total - 1) % ne
    chunk_b = (s - 1) % ne
    chunk_c = (s - 2) % ne

    @pl.when(s == 0)
    def _():
        ht_ref[...] = jnp.zeros_like(ht_ref)
        coef_ref[...] = jnp.zeros_like(coef_ref)
        th_ref[...] = jnp.zeros_like(th_ref)
        aa_ref[...] = jnp.zeros_like(aa_ref)
        s2_ref[...] = jnp.zeros_like(s2_ref)
        bb_ref[...] = jnp.zeros_like(bb_ref)
        acc_ref[...] = jnp.zeros_like(acc_ref)

    @pl.when(jnp.logical_and(s < total, chunk_a == 0))
    def _():
        xb_ref[...] = xa_ref[...].astype(BF16)

    @pl.when(jnp.logical_and(jnp.logical_and(s >= 1, s <= total), chunk_b == 0))
    def _():
        xb = xb_ref[...]
        for hd in range(PEER_HEADS):
            lo = hd * 2 * PEER_HALF
            q = _mm(xb, wq_ref[:, lo:lo + 2 * PEER_HALF]).astype(BF16)
            s1_all = _mm_nt(k1_ref[...], q[:, :PEER_HALF])
            s2_all = _mm_nt(k2_ref[...], q[:, PEER_HALF:])
            for lt in range(tb // LANES):
                ls = slice(lt * LANES, (lt + 1) * LANES)
                s1 = s1_all[:, ls]
                s2 = s2_all[:, ls]
                t1 = _top_values_sorted(s1, nk)
                t2 = _top_values_sorted(s2, nk)
                cand = [t1[r:r + 1] + t2[:nk // (r + 1)] for r in range(nk)]
                fill = -sum(c.shape[0] for c in cand) % (SUBLANES * SUBLANES)
                cand = jnp.concatenate(cand + [jnp.full((fill, LANES), NEG_INF, F32)], axis=0)
                best = _top_values_sorted(cand, nk)
                tau = 0.5 * (best[PEER_TOPK - 1:PEER_TOPK] + best[PEER_TOPK:PEER_TOPK + 1])
                z = jnp.sum(jnp.exp(best[:PEER_TOPK] - best[0:1]), axis=0, keepdims=True)
                th = tau - s1
                aa = 0.5 * jnp.exp(s1 - t1[0:1]) / z
                for grp in range(PEER_NKEYS // SUBLANES):
                    th_ref[hd, grp, :, ls] = th[grp * SUBLANES:(grp + 1) * SUBLANES]
                    aa_ref[hd, grp, :, ls] = aa[grp * SUBLANES:(grp + 1) * SUBLANES]
                s2_ref[hd, :, ls] = s2
                bb_ref[hd, :, ls] = jnp.exp(s2 - t2[0:1])

    @pl.when(jnp.logical_and(s >= 2, chunk_c == 0))
    def _():
        acc_ref[...] = jnp.zeros_like(acc_ref)

    rows_per = PEER_NKEYS
    assert ec == SUBLANES * rows_per

    def stage_a(ht_w, ts, ms):
        ht_w[ms, ts] = _mm_nt(u_ref[ms, :], xb_ref[ts, :])

    def stage_b(ht_r, coef_w, ig, jh, lt):
        ls = slice(lt * LANES, (lt + 1) * LANES)
        js = slice(jh * (rows_per // PEER_JSPLIT), (jh + 1) * (rows_per // PEER_JSPLIT))
        iis = range(ig * PEER_IGROUP, (ig + 1) * PEER_IGROUP)
        gsum = [jnp.zeros((rows_per // PEER_JSPLIT, LANES), F32) for _ in iis]
        for hd in range(PEER_HEADS):
            s2 = s2_ref[hd, js, ls]
            bb = bb_ref[hd, js, ls]
            for k, ii in enumerate(iis):
                th = th_ref[hd, chunk_b, ii:ii + 1, ls]
                aa = aa_ref[hd, chunk_b, ii:ii + 1, ls]
                gsum[k] = gsum[k] + jnp.where(s2 >= th, bb, 0.0) * aa
        for k, ii in enumerate(iis):
            rs = slice(ii * rows_per + js.start, ii * rows_per + js.stop)
            ht = ht_r[rs, ls]
            act2 = ht * (1.0 + lax.erf(ht * np.float32(np.sqrt(0.5))))
            coef_w[rs, ls] = (gsum[k] * act2).astype(BF16)

    def stage_c(coef_r, ts, ms):
        acc_ref[ms, ts] += _mm(vt_ref[0, ms, :], coef_r[:, ts])

    assert acc_ref.shape[0] == ec
    mxu_pieces = [(slice(t0, t0 + PEER_SLICE), slice(m0, m0 + PEER_MROWS))
                  for t0 in range(0, tb, PEER_SLICE) for m0 in range(0, ec, PEER_MROWS)]
    valu_pieces = [(ig, jh, lt) for lt in range(tb // LANES) for ig in range(SUBLANES // PEER_IGROUP)
                   for jh in range(PEER_JSPLIT)]
    per = -(-len(valu_pieces) // len(mxu_pieces))

    def stages(ht_w, ht_r, coef_w, coef_r):
        for idx, (ts, ms) in enumerate(mxu_pieces):
            stage_a(ht_w, ts, ms)
            stage_c(coef_r, ts, ms)
            for piece in valu_pieces[idx * per:(idx + 1) * per]:
                stage_b(ht_r, coef_w, *piece)

    slot_a = s % 2
    slot_b = (s + 1) % 2
    stages(ht_ref.at[slot_a], ht_ref.at[slot_b], coef_ref.at[slot_b], coef_ref.at[slot_a])

    @pl.when(jnp.logical_and(s >= 2, chunk_c == ne - 1))
    def _():
        y = acc_ref[...].T
        out_ref[...] = _layer_norm(alpha * xc_ref[...] + y, g_ref[...], b_ref[...])


PEER_SLICE = 256
PEER_MROWS = 256
PEER_IGROUP = 4
PEER_JSPLIT = 2


def _peer_call(x, wq, k1, k2, u, vt, g, b, tb, ec, alpha):
    n, d = x.shape
    ne = u.shape[0] // ec
    total = (n // tb) * ne
    once = pl.Buffered(1)
    full = lambda a: pl.BlockSpec(a.shape, lambda s: (0,) * a.ndim, pipeline_mode=once)
    blk_a = lambda s: jnp.clip(s, 0, total - 1) // ne
    blk_c = lambda s: jnp.clip(s - 2, 0, total - 1) // ne
    head_scratch = pltpu.VMEM((PEER_HEADS, PEER_NKEYS, tb), F32)
    grouped_scratch = pltpu.VMEM((PEER_HEADS, PEER_NKEYS // SUBLANES, SUBLANES, tb), F32)
    return pl.pallas_call(
        functools.partial(_peer_kernel, alpha=alpha, ec=ec, ne=ne, total=total),
        grid=(total + 2,),
        in_specs=[pl.BlockSpec((tb, d), lambda s: (blk_a(s), 0), pipeline_mode=once),
                  pl.BlockSpec((tb, d), lambda s: (blk_c(s), 0), pipeline_mode=once),
                  full(wq), full(k1), full(k2),
                  pl.BlockSpec((ec, d), lambda s: (jnp.clip(s, 0, total - 1) % ne, 0)),
                  pl.BlockSpec((1, d, ec), lambda s: (jnp.clip(s - 2, 0, total - 1) % ne, 0, 0)),
                  full(g), full(b)],
        out_specs=pl.BlockSpec((tb, d), lambda s: (blk_c(s), 0)),
        out_shape=jax.ShapeDtypeStruct((n, d), F32),
        scratch_shapes=[pltpu.VMEM((tb, d), BF16), pltpu.VMEM((d, tb), F32),
                        pltpu.VMEM((2, ec, tb), F32), pltpu.VMEM((2, ec, tb), BF16),
                        grouped_scratch, grouped_scratch, head_scratch, head_scratch],
        compiler_params=_params("arbitrary"),
    )(x, x, wq, k1, k2, u, vt, g, b)


def kernel(x, meta_tokens, ln_in_g, ln_in_b, w_in, conv_w, gdn_conv_w, a_log, dt_bias, gdn_norm_w, w_out, ln1_g,
           ln1_b, peer_w_q, peer_k1, peer_k2, peer_u, peer_v, ln2_g, ln2_b):
    bsz, seq, d = x.shape
    depth = w_in.shape[0]
    alpha = (2.0 * depth) ** 0.25
    lp = SEQ_PAD + N_META + seq
    n = bsz * lp
    assert lp % GDN_CHUNK == 0

    tb_row = _pick_block(n, 768, LANES)
    tb_in = _pick_block(lp, 768, 2 * SUBLANES)
    chunks = 3 if (lp // GDN_CHUNK) % 3 == 0 else 1
    chunks_local = 12 if (n // GDN_CHUNK) % 12 == 0 else chunks
    tb_peer = _pick_block(n, 768, PEER_SLICE)
    ec = SUBLANES * PEER_NKEYS
    n_exp = peer_u.shape[1]

    row = lambda a: a.reshape(1, -1).astype(F32)
    pad_lanes = lambda a: jnp.pad(a.reshape(1, -1).astype(F32), ((0, 0), (0, LANES - a.size)))

    h = _ln_in_call(x.astype(F32), meta_tokens.astype(F32), row(ln_in_g), row(ln_in_b), lp).reshape(n, d)

    c3 = 3 * CONV_WIDTH
    for l in range(depth):
        wl = w_in[l]
        wc = wl[:, :c3].astype(BF16)
        wqkv = wl[:, c3:c3 + QKV_COLS].astype(BF16)
        wz = wl[:, c3 + QKV_COLS:c3 + QKV_COLS + GDN_WIDTH].astype(BF16)
        wab = jnp.pad(wl[:, c3 + QKV_COLS + GDN_WIDTH:], ((0, 0), (0, LANES - 2 * GDN_HEADS))).astype(BF16)
        yconv, q, k, v, sz, gb = _inproj_call(
            h.reshape(bsz, lp, d), wc, wqkv, wz, wab, conv_w[l].astype(F32), gdn_conv_w[l].astype(F32),
            pad_lanes(a_log[l]), pad_lanes(dt_bias[l]), tb_in)
        flat = lambda a: a.reshape(n, a.shape[-1])
        wg, qd, kd, ug, attn, egl = _gdn_local_call(flat(q), flat(k), flat(v), flat(gb), chunks_local)
        seq = lambda a: a.reshape(bsz, lp, a.shape[-1])
        o = _gdn_scan_call(seq(wg), seq(qd), seq(kd), seq(ug), seq(attn),
                           egl.reshape(bsz, lp // GDN_CHUNK, SUBLANES, LANES), chunks)
        wo = w_out[l].astype(BF16)
        h = _outproj_call(o.reshape(n, GDN_WIDTH), sz.reshape(n, GDN_WIDTH), yconv.reshape(n, CONV_WIDTH), h,
                          wo[:CONV_WIDTH], wo[CONV_WIDTH:], row(gdn_norm_w[l]), row(ln1_g[l]), row(ln1_b[l]),
                          tb_row, alpha)
        ub = peer_u[l].astype(BF16)
        vt = peer_v[l].astype(BF16).reshape(n_exp // ec, ec, d).transpose(0, 2, 1)
        h = _peer_call(h, peer_w_q[l].astype(BF16), peer_k1[l].astype(BF16), peer_k2[l].astype(BF16), ub, vt,
                       row(ln2_g[l]), row(ln2_b[l]), tb_peer, ec, alpha)
    return h.reshape(bsz, lp, d)[:, SEQ_PAD + N_META:]
```

```python
import functools

import jax
import jax.numpy as jnp
import numpy as np
from jax import lax
from jax.experimental import pallas as pl
from jax.experimental.pallas import tpu as pltpu

F32 = jnp.float32
BF16 = jnp.bfloat16

N_META = 16
CONV_WIDTH = 512
CONV_K = 3
GDN_HEADS = 4
GDN_DK = 128
GDN_DV = 128
GDN_CONV_K = 4
GDN_CHUNK = 64
QKV_COLS = GDN_HEADS * (2 * GDN_DK + GDN_DV)
GDN_WIDTH = GDN_HEADS * GDN_DV
PEER_HEADS = 8
PEER_NKEYS = 128
PEER_HALF = 128
PEER_TOPK = 16
LN_EPS = 1e-5
RMS_EPS = 1e-6
SEQ_PAD = (-N_META) % GDN_CHUNK
LANES = 128
SUBLANES = 8
VMEM_LIMIT = 56 * 1024 * 1024
NEG_INF = float("-inf")


def _mm(a, b, precision=None):
    return jnp.dot(a, b, preferred_element_type=F32, precision=precision)


def _mm_nt(a, b, precision=None):
    return lax.dot_general(a, b, (((1,), (1,)), ((), ())), preferred_element_type=F32, precision=precision)


def _mm_tn(a, b, precision=None):
    return lax.dot_general(a, b, (((0,), (0,)), ((), ())), preferred_element_type=F32, precision=precision)


def _layer_norm(x, g, b):
    mu = jnp.mean(x, axis=-1, keepdims=True)
    xc = x - mu
    var = jnp.mean(xc * xc, axis=-1, keepdims=True)
    return xc * lax.rsqrt(var + LN_EPS) * g + b


def _silu(x):
    return x * jax.nn.sigmoid(x)


def _pick_block(n, target, mult):
    best = None
    for d in range(mult, min(n, target) + 1, mult):
        if n % d == 0:
            best = d
    assert best is not None, (n, target, mult)
    return best


def _params(*sem):
    return pltpu.CompilerParams(dimension_semantics=sem, vmem_limit_bytes=VMEM_LIMIT)


def _ln_kernel(x_ref, g_ref, b_ref, o_ref):
    o_ref[...] = _layer_norm(x_ref[...], g_ref[...], b_ref[...])


def _ln_call(x2d, g, b, tb):
    n, d = x2d.shape
    return pl.pallas_call(
        _ln_kernel,
        grid=(n // tb,),
        in_specs=[pl.BlockSpec((tb, d), lambda i: (i, 0)),
                  pl.BlockSpec((1, d), lambda i: (0, 0)),
                  pl.BlockSpec((1, d), lambda i: (0, 0))],
        out_specs=pl.BlockSpec((tb, d), lambda i: (i, 0)),
        out_shape=jax.ShapeDtypeStruct((n, d), F32),
        compiler_params=_params("parallel"),
    )(x2d, g, b)


def _causal_conv(u, prev, w):
    k = w.shape[0]

    def taps(a):
        acc = a * w[k - 1:k]
        for s in range(1, k):
            acc = acc + pltpu.roll(a, s, 0) * w[k - 1 - s:k - s]
        return acc

    body = taps(u)
    head = taps(jnp.concatenate([prev, u[:SUBLANES]], axis=0))[SUBLANES:]
    return jnp.concatenate([head, body[SUBLANES:]], axis=0)


def _inproj_kernel(h_ref, wc_ref, wqkv_ref, wz_ref, wab_ref, cw_ref, gcw_ref, alog_ref, dtb_ref,
                   yconv_ref, q_ref, k_ref, v_ref, sz_ref, gb_ref, carry_c, carry_qkv):
    j = pl.program_id(1)
    tb = h_ref.shape[1]

    @pl.when(j == 0)
    def _():
        carry_c[...] = jnp.zeros_like(carry_c)
        carry_qkv[...] = jnp.zeros_like(carry_qkv)

    xb = h_ref[0].astype(BF16)
    row = j * tb + lax.broadcasted_iota(jnp.int32, (tb, 1), 0)
    valid = row >= SEQ_PAD

    p = _mm(xb, wc_ref[...])
    u = jnp.where(valid, p[:, CONV_WIDTH:2 * CONV_WIDTH] * p[:, 2 * CONV_WIDTH:], 0.0)
    yconv_ref[0] = (p[:, :CONV_WIDTH] * _causal_conv(u, carry_c[...], cw_ref[...])).astype(BF16)
    carry_c[...] = u[tb - SUBLANES:]

    p = jnp.where(valid, _mm(xb, wqkv_ref[...]), 0.0)
    s = _silu(_causal_conv(p, carry_qkv[...], gcw_ref[...]))
    carry_qkv[...] = p[tb - SUBLANES:]
    for hh in range(GDN_HEADS):
        lo = hh * GDN_DK
        qh = s[:, lo:lo + GDN_DK]
        kh = s[:, GDN_WIDTH + lo:GDN_WIDTH + lo + GDN_DK]
        q_ref[0, :, lo:lo + GDN_DK] = qh * lax.rsqrt(jnp.sum(qh * qh, axis=-1, keepdims=True) + RMS_EPS)
        k_ref[0, :, lo:lo + GDN_DK] = kh * lax.rsqrt(jnp.sum(kh * kh, axis=-1, keepdims=True) + RMS_EPS)
    v_ref[0] = s[:, 2 * GDN_WIDTH:]

    sz_ref[0] = _silu(_mm(xb, wz_ref[...])).astype(BF16)

    ab = _mm(xb, wab_ref[...])
    t = ab + dtb_ref[...]
    softplus = jnp.maximum(t, 0.0) + jnp.log1p(jnp.exp(-jnp.abs(t)))
    g = -jnp.exp(alog_ref[...]) * softplus
    lane = lax.broadcasted_iota(jnp.int32, (1, LANES), 1)
    gb = jnp.where(lane < GDN_HEADS, g, jax.nn.sigmoid(ab))
    gb_ref[0] = jnp.where(valid, gb, 0.0)


def _inproj_call(h, wc, wqkv, wz, wab, cw, gcw, alog, dtb, tb):
    b, lp, d = h.shape
    full = lambda a: pl.BlockSpec(a.shape, lambda bi, j: (0,) * a.ndim)
    blk = lambda w: pl.BlockSpec((1, tb, w), lambda bi, j: (bi, j, 0))
    out = lambda w, dt: jax.ShapeDtypeStruct((b, lp, w), dt)
    return pl.pallas_call(
        _inproj_kernel,
        grid=(b, lp // tb),
        in_specs=[blk(d), full(wc), full(wqkv), full(wz), full(wab), full(cw), full(gcw), full(alog), full(dtb)],
        out_specs=[blk(CONV_WIDTH), blk(GDN_WIDTH), blk(GDN_WIDTH), blk(GDN_WIDTH), blk(GDN_WIDTH), blk(LANES)],
        out_shape=[out(CONV_WIDTH, BF16), out(GDN_WIDTH, F32), out(GDN_WIDTH, F32), out(GDN_WIDTH, F32),
                   out(GDN_WIDTH, BF16), out(LANES, F32)],
        scratch_shapes=[pltpu.VMEM((SUBLANES, CONV_WIDTH), F32), pltpu.VMEM((SUBLANES, QKV_COLS), F32)],
        compiler_params=_params("parallel", "arbitrary"),
    )(h, wc, wqkv, wz, wab, cw, gcw, alog, dtb)


GDN_BASE = 16


def _gdn_local_kernel(q_ref, k_ref, v_ref, gb_ref, w_ref, qd_ref, kd_ref, u_ref, attn_ref, egl_ref, *, chunks):
    c = GDN_CHUNK
    hi = lax.Precision.HIGHEST
    rows_all = chunks * c

    ri = lax.broadcasted_iota(jnp.int32, (c, c), 0)
    ci = lax.broadcasted_iota(jnp.int32, (c, c), 1)
    causal = ri >= ci
    strict = ri > ci
    eye = (ri == ci).astype(F32)
    base = jnp.logical_and(strict, (ri // GDN_BASE) == (ci // GDN_BASE))
    levels = [jnp.logical_and(jnp.logical_and((ri >> (l + 1)) == (ci >> (l + 1)), ((ri >> l) & 1) == 1),
                              ((ci >> l) & 1) == 0) for l in range(GDN_BASE.bit_length() - 1, c.bit_length() - 1)]
    sel = (lax.broadcasted_iota(jnp.int32, (SUBLANES, LANES), 0)
           == lax.broadcasted_iota(jnp.int32, (SUBLANES, LANES), 1)).astype(F32)

    gb_all = gb_ref[...]
    pos = lax.broadcasted_iota(jnp.int32, (rows_all, 1), 0) % c
    gcum_all = gb_all
    for sh in (1 << k for k in range(c.bit_length() - 1)):
        gcum_all = gcum_all + jnp.where(pos >= sh, pltpu.roll(gcum_all, sh, 0), 0.0)
    gcum_t_all = _mm_nt(sel, gcum_all, hi)

    for cc in range(chunks):
        last = cc * c + c - 1
        egl_ref[cc] = jnp.exp(jnp.broadcast_to(gcum_t_all[:, last:last + 1], (SUBLANES, LANES)))

    items = [(cc, hh) for cc in range(chunks) for hh in range(GDN_HEADS)]
    each = lambda fn, *lists: [fn(*args) for args in zip(*lists)]
    rows_of = [slice(cc * c, (cc + 1) * c) for cc, _ in items]
    cols_of = [slice(hh * GDN_DK, (hh + 1) * GDN_DK) for _, hh in items]

    kh = each(lambda r, l: k_ref[r, l], rows_of, cols_of)
    khb = each(lambda x: x.astype(BF16), kh)
    beta = [gb_all[r, GDN_HEADS + hh:GDN_HEADS + hh + 1] for r, (_, hh) in zip(rows_of, items)]
    gcol = [gcum_all[r, hh:hh + 1] for r, (_, hh) in zip(rows_of, items)]
    grow = [gcum_t_all[hh:hh + 1, r] for r, (_, hh) in zip(rows_of, items)]
    decay = each(lambda gc_, gr_: jnp.where(causal, jnp.exp(jnp.where(causal, gc_ - gr_, 0.0)), 0.0), gcol, grow)
    kbeta = each(lambda x, b_: x * b_, kh, beta)
    a = each(lambda kb_, k_, d_: jnp.where(strict, _mm_nt(kb_.astype(BF16), k_) * d_, 0.0), kbeta, khb, decay)
    n1 = each(lambda a_: jnp.where(base, -a_, 0.0), a)
    n1b = each(lambda x: x.astype(BF16), n1)
    n2 = each(lambda x: _mm(x, x), n1b)
    npow = each(lambda x: x.astype(BF16), n2)
    tinv = each(lambda n1_, n2_, n1b_, n2b_: eye + n1_ + n2_ + _mm(n1b_, n2b_), n1, n2, n1b, npow)
    for _ in range(GDN_BASE.bit_length() - 3):
        npow = each(lambda x: _mm(x, x).astype(BF16), npow)
        tinv = each(lambda t_, p_: t_ + _mm(t_.astype(BF16), p_), tinv, npow)
    for m in levels:
        tb16 = each(lambda t_: t_.astype(BF16), tinv)
        mt = each(lambda a_, t_: _mm(jnp.where(m, a_, 0.0).astype(BF16), t_).astype(BF16), a, tb16)
        tinv = each(lambda t_, tb_, mt_: t_ - _mm(tb_, mt_), tinv, tb16, mt)
    tb16 = each(lambda t_: t_.astype(BF16), tinv)
    egc = each(jnp.exp, gcol)
    vh = each(lambda r, l: v_ref[r, l], rows_of, cols_of)
    rhs = each(lambda v_, b_, kb_, e_: jnp.concatenate([v_ * b_, kb_ * e_], axis=1).astype(BF16),
               vh, beta, kbeta, egc)
    uw = each(_mm, tb16, rhs)
    qh = each(lambda r, l: q_ref[r, l] * (GDN_DK ** -0.5), rows_of, cols_of)
    attn = each(lambda q_, k_, d_: jnp.where(causal, _mm_nt(q_.astype(BF16), k_) * d_, 0.0).astype(BF16),
                qh, khb, decay)
    for idx, (cc, hh) in enumerate(items):
        r, l = rows_of[idx], cols_of[idx]
        u_ref[r, l] = uw[idx][:, :GDN_DV]
        w_ref[r, l] = uw[idx][:, GDN_DV:].astype(BF16)
        attn_ref[r, hh * c:(hh + 1) * c] = attn[idx]
        glast = gcol[idx][c - 1:c, :]
        kd_ref[r, l] = (kh[idx] * jnp.exp(glast - gcol[idx])).astype(BF16)
        qd_ref[r, l] = (qh[idx] * egc[idx]).astype(BF16)


def _gdn_local_call(q, k, v, gb, chunks):
    n, w = q.shape
    tb = chunks * GDN_CHUNK
    blk = lambda width: pl.BlockSpec((tb, width), lambda i: (i, 0))
    out = lambda width, dt: jax.ShapeDtypeStruct((n, width), dt)
    return pl.pallas_call(
        functools.partial(_gdn_local_kernel, chunks=chunks),
        grid=(n // tb,),
        in_specs=[blk(w), blk(w), blk(w), blk(LANES)],
        out_specs=[blk(w), blk(w), blk(w), blk(w), blk(GDN_HEADS * GDN_CHUNK),
                   pl.BlockSpec((chunks, SUBLANES, LANES), lambda i: (i, 0, 0))],
        out_shape=[out(w, BF16), out(w, BF16), out(w, BF16), out(w, F32), out(GDN_HEADS * GDN_CHUNK, BF16),
                   jax.ShapeDtypeStruct((n // GDN_CHUNK, SUBLANES, LANES), F32)],
        compiler_params=_params("parallel"),
    )(q, k, v, gb)


def _gdn_scan_kernel(w_ref, qd_ref, kd_ref, u_ref, attn_ref, egl_ref, o_ref, state, *, chunks):
    c = GDN_CHUNK
    bsz = w_ref.shape[0]

    @pl.when(pl.program_id(0) == 0)
    def _():
        state[...] = jnp.zeros_like(state)

    items = [(b, hh) for b in range(bsz) for hh in range(GDN_HEADS)]
    each = lambda fn, *lists: [fn(*args) for args in zip(*lists)]
    cols_of = [slice(hh * GDN_DK, (hh + 1) * GDN_DK) for _, hh in items]
    sh = [state[idx] for idx in range(len(items))]
    for cc in range(chunks):
        rows = slice(cc * c, (cc + 1) * c)
        sb = each(lambda s_: s_.astype(BF16), sh)
        wq = [jnp.concatenate([w_ref[b, rows, l], qd_ref[b, rows, l]], axis=0) for (b, _), l in zip(items, cols_of)]
        r = each(_mm, wq, sb)
        vnb = [(u_ref[b, rows, l] - r_[:c]).astype(BF16) for (b, _), l, r_ in zip(items, cols_of, r)]
        for (b, hh), l, r_, v_ in zip(items, cols_of, r, vnb):
            o_ref[b, rows, l] = r_[c:] + _mm(attn_ref[b, rows, hh * c:(hh + 1) * c], v_)
        sh = [s_ * egl_ref[b, cc, hh:hh + 1, :] + _mm_tn(kd_ref[b, rows, l], v_)
              for (b, hh), l, s_, v_ in zip(items, cols_of, sh, vnb)]
    for idx, s_ in enumerate(sh):
        state[idx] = s_


def _gdn_scan_call(w, qd, kd, u, attn, egl, chunks):
    b, lp, width = u.shape
    tb = chunks * GDN_CHUNK
    blk = lambda wd: pl.BlockSpec((b, tb, wd), lambda j: (0, j, 0))
    return pl.pallas_call(
        functools.partial(_gdn_scan_kernel, chunks=chunks),
        grid=(lp // tb,),
        in_specs=[blk(width), blk(width), blk(width), blk(width), blk(GDN_HEADS * GDN_CHUNK),
                  pl.BlockSpec((b, chunks, SUBLANES, LANES), lambda j: (0, j, 0, 0))],
        out_specs=blk(width),
        out_shape=jax.ShapeDtypeStruct((b, lp, width), F32),
        scratch_shapes=[pltpu.VMEM((b * GDN_HEADS, GDN_DK, GDN_DV), F32)],
        compiler_params=_params("arbitrary"),
    )(w, qd, kd, u, attn, egl)


def _outproj_kernel(o_ref, sz_ref, yconv_ref, h_ref, wtop_ref, wbot_ref, nw_ref, g_ref, b_ref, out_ref, *, alpha):
    o = o_ref[...]
    parts = []
    for hh in range(GDN_HEADS):
        oh = o[:, hh * GDN_DV:(hh + 1) * GDN_DV]
        parts.append(oh * lax.rsqrt(jnp.mean(oh * oh, axis=-1, keepdims=True) + RMS_EPS) * nw_ref[...])
    ygdn = (jnp.concatenate(parts, axis=-1) * sz_ref[...].astype(F32)).astype(BF16)
    mix = _mm(yconv_ref[...], wtop_ref[...]) + _mm(ygdn, wbot_ref[...])
    out_ref[...] = _layer_norm(alpha * h_ref[...] + mix, g_ref[...], b_ref[...])


def _outproj_call(o, sz, yconv, h, wtop, wbot, nw, g, b, tb, alpha):
    n, d = h.shape
    full = lambda a: pl.BlockSpec(a.shape, lambda i: (0,) * a.ndim)
    blk = lambda w: pl.BlockSpec((tb, w), lambda i: (i, 0))
    return pl.pallas_call(
        functools.partial(_outproj_kernel, alpha=alpha),
        grid=(n // tb,),
        in_specs=[blk(GDN_WIDTH), blk(GDN_WIDTH), blk(CONV_WIDTH), blk(d), full(wtop), full(wbot), full(nw),
                  full(g), full(b)],
        out_specs=blk(d),
        out_shape=jax.ShapeDtypeStruct((n, d), F32),
        compiler_params=_params("parallel"),
    )(o, sz, yconv, h, wtop, wbot, nw, g, b)


def _sorting_network(n):
    pairs = []
    p = 1
    while p < n:
        k = p
        while k >= 1:
            for j in range(k % p, n - k, 2 * k):
                for i in range(min(k, n - j - k)):
                    if (i + j) // (2 * p) == (i + j + k) // (2 * p):
                        pairs.append((i + j, i + j + k))
            k //= 2
        p *= 2
    return pairs


def _top_values_sorted(s, n):
    groups = s.shape[0] // SUBLANES
    cols = [s[g * SUBLANES:(g + 1) * SUBLANES] for g in range(groups)]
    for a, b in _sorting_network(groups):
        cols[a], cols[b] = jnp.maximum(cols[a], cols[b]), jnp.minimum(cols[a], cols[b])
    tops = []
    for r in range(n):
        m = jnp.max(cols[0], axis=0, keepdims=True)
        tops.append(m)
        if r + 1 < n:
            hit = cols[0] >= m
            for k in range(min(groups, n - 1 - r)):
                below = cols[k + 1] if k + 1 < groups else NEG_INF
                cols[k] = jnp.where(hit, below, cols[k])
    return jnp.concatenate(tops, axis=0)


def _peer_kernel(xa_ref, xc_ref, wq_ref, k1_ref, k2_ref, u_ref, vt_ref, g_ref, b_ref, out_ref,
                 xb_ref, acc_ref, ht_ref, coef_ref, th_ref, aa_ref, s2_ref, bb_ref, *, alpha, ec, ne, total):
    s = pl.program_id(0)
    tb = xa_ref.shape[0]
    nk = PEER_TOPK + 1
    chunk_a = jnp.clip(s, 0, total - 1) % ne
    chunk_b = (s - 1) % ne
    chunk_c = (s - 2) % ne

    @pl.when(s == 0)
    def _():
        ht_ref[...] = jnp.zeros_like(ht_ref)
        coef_ref[...] = jnp.zeros_like(coef_ref)
        th_ref[...] = jnp.zeros_like(th_ref)
        aa_ref[...] = jnp.zeros_like(aa_ref)
        s2_ref[...] = jnp.zeros_like(s2_ref)
        bb_ref[...] = jnp.zeros_like(bb_ref)
        acc_ref[...] = jnp.zeros_like(acc_ref)

    @pl.when(jnp.logical_and(s < total, chunk_a == 0))
    def _():
        xb_ref[...] = xa_ref[...].astype(BF16)

    @pl.when(jnp.logical_and(jnp.logical_and(s >= 1, s <= total), chunk_b == 0))
    def _():
        xb = xb_ref[...]
        for hd in range(PEER_HEADS):
            lo = hd * 2 * PEER_HALF
            q = _mm(xb, wq_ref[:, lo:lo + 2 * PEER_HALF]).astype(BF16)
            s1_all = _mm_nt(k1_ref[...], q[:, :PEER_HALF])
            s2_all = _mm_nt(k2_ref[...], q[:, PEER_HALF:])
            for lt in range(tb // LANES):
                ls = slice(lt * LANES, (lt + 1) * LANES)
                s1 = s1_all[:, ls]
                s2 = s2_all[:, ls]
                t1 = _top_values_sorted(s1, nk)
                t2 = _top_values_sorted(s2, nk)
                cand = [t1[r:r + 1] + t2[:nk // (r + 1)] for r in range(nk)]
                fill = -sum(c.shape[0] for c in cand) % (SUBLANES * SUBLANES)
                cand = jnp.concatenate(cand + [jnp.full((fill, LANES), NEG_INF, F32)], axis=0)
                best = _top_values_sorted(cand, nk)
                tau = 0.5 * (best[PEER_TOPK - 1:PEER_TOPK] + best[PEER_TOPK:PEER_TOPK + 1])
                z = jnp.sum(jnp.exp(best[:PEER_TOPK] - best[0:1]), axis=0, keepdims=True)
                th = tau - s1
                aa = 0.5 * jnp.exp(s1 - t1[0:1]) / z
                for grp in range(PEER_NKEYS // SUBLANES):
                    th_ref[hd, grp, :, ls] = th[grp * SUBLANES:(grp + 1) * SUBLANES]
                    aa_ref[hd, grp, :, ls] = aa[grp * SUBLANES:(grp + 1) * SUBLANES]
                s2_ref[hd, :, ls] = s2
                bb_ref[hd, :, ls] = jnp.exp(s2 - t2[0:1])

    @pl.when(jnp.logical_and(s >= 2, chunk_c == 0))
    def _():
        acc_ref[...] = jnp.zeros_like(acc_ref)

    rows_per = PEER_NKEYS
    assert ec == SUBLANES * rows_per

    def stage_a(ht_w, ts, ms):
        ht_w[ms, ts] = _mm_nt(u_ref[ms, :], xb_ref[ts, :])

    def stage_b(ht_r, coef_w, ig, jh, lt):
        ls = slice(lt * LANES, (lt + 1) * LANES)
        js = slice(jh * (rows_per // PEER_JSPLIT), (jh + 1) * (rows_per // PEER_JSPLIT))
        iis = range(ig * PEER_IGROUP, (ig + 1) * PEER_IGROUP)
        gsum = [jnp.zeros((rows_per // PEER_JSPLIT, LANES), F32) for _ in iis]
        for hd in range(PEER_HEADS):
            s2 = s2_ref[hd, js, ls]
            bb = bb_ref[hd, js, ls]
            for k, ii in enumerate(iis):
                th = th_ref[hd, chunk_b, ii:ii + 1, ls]
                aa = aa_ref[hd, chunk_b, ii:ii + 1, ls]
                gsum[k] = gsum[k] + jnp.where(s2 >= th, bb, 0.0) * aa
        for k, ii in enumerate(iis):
            rs = slice(ii * rows_per + js.start, ii * rows_per + js.stop)
            ht = ht_r[rs, ls]
            act2 = ht * (1.0 + lax.erf(ht * np.float32(np.sqrt(0.5))))
            coef_w[rs, ls] = (gsum[k] * act2).astype(BF16)

    def stage_c(coef_r, ts, ms):
        acc_ref[ms, ts] += _mm(vt_ref[0, ms, :], coef_r[:, ts])

    assert acc_ref.shape[0] == ec
    mxu_pieces = [(slice(t0, t0 + PEER_SLICE), slice(m0, m0 + PEER_MROWS))
                  for t0 in range(0, tb, PEER_SLICE) for m0 in range(0, ec, PEER_MROWS)]
    valu_pieces = [(ig, jh, lt) for lt in range(tb // LANES) for ig in range(SUBLANES // PEER_IGROUP)
                   for jh in range(PEER_JSPLIT)]
    per = -(-len(valu_pieces) // len(mxu_pieces))

    def stages(ht_w, ht_r, coef_w, coef_r):
        for idx, (ts, ms) in enumerate(mxu_pieces):
            stage_a(ht_w, ts, ms)
            stage_c(coef_r, ts, ms)
            for piece in valu_pieces[idx * per:(idx + 1) * per]:
                stage_b(ht_r, coef_w, *piece)

    slot_a = s % 2
    slot_b = (s + 1) % 2
    stages(ht_ref.at[slot_a], ht_ref.at[slot_b], coef_ref.at[slot_b], coef_ref.at[slot_a])

    @pl.when(jnp.logical_and(s >= 2, chunk_c == ne - 1))
    def _():
        for t0 in range(0, tb, LANES):
            ts = slice(t0, t0 + LANES)
            y = acc_ref[:, ts].T
            out_ref[ts, :] = _layer_norm(alpha * xc_ref[ts, :] + y, g_ref[...], b_ref[...])


PEER_SLICE = 256
PEER_MROWS = 256
PEER_IGROUP = 4
PEER_JSPLIT = 2


def _peer_call(x, wq, k1, k2, u, vt, g, b, tb, ec, alpha):
    n, d = x.shape
    ne = u.shape[0] // ec
    total = (n // tb) * ne
    once = pl.Buffered(1)
    full = lambda a: pl.BlockSpec(a.shape, lambda s: (0,) * a.ndim, pipeline_mode=once)
    blk_a = lambda s: jnp.clip(s, 0, total - 1) // ne
    blk_c = lambda s: jnp.clip(s - 2, 0, total - 1) // ne
    head_scratch = pltpu.VMEM((PEER_HEADS, PEER_NKEYS, tb), F32)
    grouped_scratch = pltpu.VMEM((PEER_HEADS, PEER_NKEYS // SUBLANES, SUBLANES, tb), F32)
    return pl.pallas_call(
        functools.partial(_peer_kernel, alpha=alpha, ec=ec, ne=ne, total=total),
        grid=(total + 2,),
        in_specs=[pl.BlockSpec((tb, d), lambda s: (blk_a(s), 0), pipeline_mode=once),
                  pl.BlockSpec((tb, d), lambda s: (blk_c(s), 0), pipeline_mode=once),
                  full(wq), full(k1), full(k2),
                  pl.BlockSpec((ec, d), lambda s: (jnp.clip(s, 0, total - 1) % ne, 0)),
                  pl.BlockSpec((1, d, ec), lambda s: (jnp.clip(s - 2, 0, total - 1) % ne, 0, 0)),
                  full(g), full(b)],
        out_specs=pl.BlockSpec((tb, d), lambda s: (blk_c(s), 0)),
        out_shape=jax.ShapeDtypeStruct((n, d), F32),
        scratch_shapes=[pltpu.VMEM((tb, d), BF16), pltpu.VMEM((d, tb), F32),
                        pltpu.VMEM((2, ec, tb), F32), pltpu.VMEM((2, ec, tb), BF16),
                        grouped_scratch, grouped_scratch, head_scratch, head_scratch],
        compiler_params=_params("arbitrary"),
    )(x, x, wq, k1, k2, u, vt, g, b)


def kernel(x, meta_tokens, ln_in_g, ln_in_b, w_in, conv_w, gdn_conv_w, a_log, dt_bias, gdn_norm_w, w_out, ln1_g,
           ln1_b, peer_w_q, peer_k1, peer_k2, peer_u, peer_v, ln2_g, ln2_b):
    bsz, seq, d = x.shape
    depth = w_in.shape[0]
    alpha = (2.0 * depth) ** 0.25
    lp = SEQ_PAD + N_META + seq
    n = bsz * lp
    assert lp % GDN_CHUNK == 0

    tb_row = _pick_block(n, 768, LANES)
    tb_in = _pick_block(lp, 768, 2 * SUBLANES)
    chunks = 3 if (lp // GDN_CHUNK) % 3 == 0 else 1
    chunks_local = 12 if (n // GDN_CHUNK) % 12 == 0 else chunks
    tb_peer = _pick_block(n, 768, PEER_SLICE)
    ec = SUBLANES * PEER_NKEYS
    n_exp = peer_u.shape[1]

    row = lambda a: a.reshape(1, -1).astype(F32)
    pad_lanes = lambda a: jnp.pad(a.reshape(1, -1).astype(F32), ((0, 0), (0, LANES - a.size)))

    meta = jnp.broadcast_to(meta_tokens[None].astype(x.dtype), (bsz, N_META, d))
    h = jnp.concatenate([jnp.zeros((bsz, SEQ_PAD, d), x.dtype), meta, x], axis=1).reshape(n, d)
    h = _ln_call(h, row(ln_in_g), row(ln_in_b), tb_row)

    c3 = 3 * CONV_WIDTH
    for l in range(depth):
        wl = w_in[l]
        wc = wl[:, :c3].astype(BF16)
        wqkv = wl[:, c3:c3 + QKV_COLS].astype(BF16)
        wz = wl[:, c3 + QKV_COLS:c3 + QKV_COLS + GDN_WIDTH].astype(BF16)
        wab = jnp.pad(wl[:, c3 + QKV_COLS + GDN_WIDTH:], ((0, 0), (0, LANES - 2 * GDN_HEADS))).astype(BF16)
        yconv, q, k, v, sz, gb = _inproj_call(
            h.reshape(bsz, lp, d), wc, wqkv, wz, wab, conv_w[l].astype(F32), gdn_conv_w[l].astype(F32),
            pad_lanes(a_log[l]), pad_lanes(dt_bias[l]), tb_in)
        flat = lambda a: a.reshape(n, a.shape[-1])
        wg, qd, kd, ug, attn, egl = _gdn_local_call(flat(q), flat(k), flat(v), flat(gb), chunks_local)
        seq = lambda a: a.reshape(bsz, lp, a.shape[-1])
        o = _gdn_scan_call(seq(wg), seq(qd), seq(kd), seq(ug), seq(attn),
                           egl.reshape(bsz, lp // GDN_CHUNK, SUBLANES, LANES), chunks)
        wo = w_out[l].astype(BF16)
        h = _outproj_call(o.reshape(n, GDN_WIDTH), sz.reshape(n, GDN_WIDTH), yconv.reshape(n, CONV_WIDTH), h,
                          wo[:CONV_WIDTH], wo[CONV_WIDTH:], row(gdn_norm_w[l]), row(ln1_g[l]), row(ln1_b[l]),
                          tb_row, alpha)
        ub = peer_u[l].astype(BF16)
        vt = peer_v[l].astype(BF16).reshape(n_exp // ec, ec, d).transpose(0, 2, 1)
        h = _peer_call(h, peer_w_q[l].astype(BF16), peer_k1[l].astype(BF16), peer_k2[l].astype(BF16), ub, vt,
                       row(ln2_g[l]), row(ln2_b[l]), tb_peer, ec, alpha)
    return h.reshape(bsz, lp, d)[:, SEQ_PAD + N_META:]
```
